```python
import jax, jax.numpy as jnp
from jax import lax
import numpy as np

D_MODEL = 4096
BATCH = 2
SEQ = 4096
DEPTH = 2

CHUNK = 64
N_MIXERS = 2
N_LAYERS_A = (DEPTH + 1) // 2
N_LAYERS_B = DEPTH // 2
GDN_K_HEADS = 16
GDN_V_HEADS = 32
GDN_HEAD_DIM = 128
GDN_KEY_DIM = GDN_K_HEADS * GDN_HEAD_DIM
GDN_VALUE_DIM = GDN_V_HEADS * GDN_HEAD_DIM
GDN_CONV_WIDTH = 4
GDN_CONV_DIM = 2 * GDN_KEY_DIM + GDN_VALUE_DIM
GDN_IN_DIM = GDN_CONV_DIM + GDN_VALUE_DIM + 2 * GDN_V_HEADS
CONF_INNER = D_MODEL
CONF_KERNEL = 31
N_EXPERTS = 16
N_GROUPS = 4
EXPERTS_PER_GROUP = N_EXPERTS // N_GROUPS
TOP_K = 2
D_FF_EXPERT = 1024
N_MOD = 6
EPS = 1e-6

kernel_name = "hybrid_gdn_conformer_grouped_moe_adaln"


def _rms_norm(x, g):
    xf = x.astype(jnp.float32)
    y = xf * lax.rsqrt(jnp.mean(xf * xf, axis=-1, keepdims=True) + EPS)
    return (y * g.astype(jnp.float32)).astype(x.dtype)


def _layer_norm(x, g, b):
    xf = x.astype(jnp.float32)
    mu = jnp.mean(xf, axis=-1, keepdims=True)
    var = jnp.mean(jnp.square(xf - mu), axis=-1, keepdims=True)
    y = (xf - mu) * lax.rsqrt(var + EPS)
    return (y * g.astype(jnp.float32) + b.astype(jnp.float32)).astype(x.dtype)


def _l2_normalize(x):
    xf = x.astype(jnp.float32)
    return xf * lax.rsqrt(jnp.sum(xf * xf, axis=-1, keepdims=True) + EPS)


def _causal_depthwise_conv(x, w):
    width, ch = w.shape
    return lax.conv_general_dilated(
        x, w[:, None, :].astype(x.dtype), window_strides=(1,),
        padding=((width - 1, 0),), dimension_numbers=("NWC", "WIO", "NWC"),
        feature_group_count=ch)


def _chunk_gated_delta_rule(q, k, v, g, beta):
    bsz, seq, heads, dh = q.shape
    n_chunks = seq // CHUNK

    def to_chunks(t):
        t = t.reshape((bsz, n_chunks, CHUNK, heads) + t.shape[3:])
        return jnp.moveaxis(t, 3, 1)

    q, k, v, g, beta = (to_chunks(t) for t in (q, k, v, g, beta))
    g = jnp.cumsum(g, axis=-1)
    causal = jnp.tril(jnp.ones((CHUNK, CHUNK), dtype=bool))
    strict = jnp.tril(jnp.ones((CHUNK, CHUNK), dtype=bool), k=-1)
    gdiff = g[..., :, None] - g[..., None, :]
    decay = jnp.where(causal, jnp.exp(jnp.where(causal, gdiff, 0.0)), 0.0)
    k_beta = k * beta[..., None]
    kk = jnp.einsum("bhnid,bhnjd->bhnij", k_beta, k)
    a_mat = jnp.eye(CHUNK, dtype=jnp.float32) + jnp.where(strict, kk * decay, 0.0)
    rhs = jnp.concatenate([v * beta[..., None], k_beta * jnp.exp(g)[..., None]], axis=-1)
    sol = lax.linalg.triangular_solve(a_mat, rhs, left_side=True, lower=True)
    u, w = sol[..., :dh], sol[..., dh:]
    qk = jnp.einsum("bhnid,bhnjd->bhnij", q, k) * decay
    q_dec = q * jnp.exp(g)[..., None]
    k_dec = k * jnp.exp(g[..., -1:] - g)[..., None]
    g_last = jnp.exp(g[..., -1])

    def step(state, inp):
        q_c, k_c, u_c, w_c, qk_c, gl_c = inp
        v_new = u_c - jnp.einsum("bhcd,bhde->bhce", w_c, state)
        o_c = (jnp.einsum("bhcd,bhde->bhce", q_c, state)
               + jnp.einsum("bhij,bhje->bhie", qk_c, v_new))
        state = state * gl_c[..., None, None] + jnp.einsum("bhcd,bhce->bhde", k_c, v_new)
        return state, o_c

    xs = tuple(jnp.moveaxis(t, 2, 0) for t in (q_dec, k_dec, u, w, qk, g_last))
    state0 = jnp.zeros((bsz, heads, dh, dh), jnp.float32)
    _, o = lax.scan(step, state0, xs)
    o = jnp.transpose(o, (1, 0, 3, 2, 4))
    return o.reshape(bsz, seq, heads, dh)


def _gated_deltanet(h, w_in, conv_w, a_log, dt_bias, norm_g, w_out):
    bsz, seq, _ = h.shape
    proj = h @ w_in
    o_z = GDN_CONV_DIM
    o_b = o_z + GDN_VALUE_DIM
    o_a = o_b + GDN_V_HEADS
    qkv = jax.nn.silu(_causal_depthwise_conv(proj[..., :o_z], conv_w))
    z = proj[..., o_z:o_b]
    b = proj[..., o_b:o_a]
    a = proj[..., o_a:]
    rep = GDN_V_HEADS // GDN_K_HEADS
    q = qkv[..., :GDN_KEY_DIM].reshape(bsz, seq, GDN_K_HEADS, GDN_HEAD_DIM)
    k = qkv[..., GDN_KEY_DIM:2 * GDN_KEY_DIM].reshape(bsz, seq, GDN_K_HEADS, GDN_HEAD_DIM)
    v = qkv[..., 2 * GDN_KEY_DIM:].reshape(bsz, seq, GDN_V_HEADS, GDN_HEAD_DIM).astype(jnp.float32)
    q = jnp.repeat(_l2_normalize(q), rep, axis=2) * (GDN_HEAD_DIM ** -0.5)
    k = jnp.repeat(_l2_normalize(k), rep, axis=2)
    beta = jax.nn.sigmoid(b.astype(jnp.float32))
    g = -jnp.exp(a_log.astype(jnp.float32)) * jax.nn.softplus(
        a.astype(jnp.float32) + dt_bias.astype(jnp.float32))
    o = _chunk_gated_delta_rule(q, k, v, g, beta)
    o = _rms_norm(o, norm_g) * jax.nn.silu(
        z.reshape(bsz, seq, GDN_V_HEADS, GDN_HEAD_DIM).astype(jnp.float32))
    return o.reshape(bsz, seq, GDN_VALUE_DIM).astype(h.dtype) @ w_out


def _conformer_conv(h, w_in, b_in, dw_w, dw_b, ln_g, ln_b, w_out, b_out):
    u = h @ w_in + b_in
    val, gate = u[..., :CONF_INNER], u[..., CONF_INNER:]
    u = val * jax.nn.sigmoid(gate)
    u = _causal_depthwise_conv(u, dw_w) + dw_b
    u = jax.nn.silu(_layer_norm(u, ln_g, ln_b))
    return u @ w_out + b_out


def _grouped_moe(h, router_w, router_bias, w_gate, w_up, w_down):
    bsz, seq, d = h.shape
    t = h.reshape(bsz * seq, d)
    logits = jnp.dot(t.astype(jnp.float32), router_w.astype(jnp.float32))
    affinity = jax.nn.sigmoid(logits)
    biased = (affinity + router_bias.astype(jnp.float32)).reshape(-1, N_GROUPS, EXPERTS_PER_GROUP)
    group_score = jnp.sum(lax.top_k(biased, TOP_K)[0], axis=-1)
    best_group = jnp.argmax(group_score, axis=-1)
    in_group = best_group[:, None] == jnp.arange(N_GROUPS)[None, :]
    masked = jnp.where(in_group[..., None], biased, -jnp.inf).reshape(-1, N_EXPERTS)
    _, idx = lax.top_k(masked, TOP_K)
    wsel = jnp.take_along_axis(affinity, idx, axis=-1)
    wsel = wsel / jnp.sum(wsel, axis=-1, keepdims=True)
    gates = jnp.einsum("tk,tke->te", wsel,
                       jax.nn.one_hot(idx, N_EXPERTS, dtype=jnp.float32)).astype(h.dtype)
    out = jnp.zeros_like(t)
    for e in range(N_EXPERTS):
        y = (jax.nn.silu(t @ w_gate[e]) * (t @ w_up[e])) @ w_down[e]
        out = out + gates[:, e:e + 1] * y
    return out.reshape(bsz, seq, d)


def setup_inputs(seed: int = 0) -> dict:
    key = jax.random.key(seed)
    ks = jax.random.split(key, 32)
    f32 = jnp.float32
    d = D_MODEL

    def nrm(k, shape, scale):
        return jax.random.normal(k, shape, f32) * scale

    dt = jnp.exp(jax.random.uniform(ks[8], (N_LAYERS_A, GDN_V_HEADS), f32)
                 * (np.log(0.1) - np.log(0.001)) + np.log(0.001))
    return {
        "x": nrm(ks[0], (BATCH, SEQ, d), 1.0),
        "c": nrm(ks[1], (BATCH, d), 1.0),
        "ada_w": nrm(ks[2], (DEPTH, d, N_MOD * d), 0.5 * d ** -0.5),
        "ada_b": nrm(ks[3], (DEPTH, N_MOD * d), 0.02),
        "norm_g": 1.0 + nrm(ks[4], (DEPTH, 2, d), 0.02),
        "gdn_w_in": nrm(ks[5], (N_LAYERS_A, d, GDN_IN_DIM), d ** -0.5),
        "gdn_conv_w": nrm(ks[6], (N_LAYERS_A, GDN_CONV_WIDTH, GDN_CONV_DIM), GDN_CONV_WIDTH ** -0.5),
        "gdn_a_log": jnp.log(jax.random.uniform(ks[7], (N_LAYERS_A, GDN_V_HEADS), f32, 1.0, 16.0)),
        "gdn_dt_bias": dt + jnp.log(-jnp.expm1(-dt)),
        "gdn_norm_g": 1.0 + nrm(ks[9], (N_LAYERS_A, GDN_HEAD_DIM), 0.02),
        "gdn_w_out": nrm(ks[10], (N_LAYERS_A, GDN_VALUE_DIM, d), GDN_VALUE_DIM ** -0.5),
        "conf_w_in": nrm(ks[11], (N_LAYERS_B, d, 2 * CONF_INNER), d ** -0.5),
        "conf_b_in": nrm(ks[12], (N_LAYERS_B, 2 * CONF_INNER), 0.02),
        "conf_dw_w": nrm(ks[13], (N_LAYERS_B, CONF_KERNEL, CONF_INNER), CONF_KERNEL ** -0.5),
        "conf_dw_b": nrm(ks[14], (N_LAYERS_B, CONF_INNER), 0.02),
        "conf_ln_g": 1.0 + nrm(ks[15], (N_LAYERS_B, CONF_INNER), 0.02),
        "conf_ln_b": nrm(ks[16], (N_LAYERS_B, CONF_INNER), 0.02),
        "conf_w_out": nrm(ks[17], (N_LAYERS_B, CONF_INNER, d), CONF_INNER ** -0.5),
        "conf_b_out": nrm(ks[18], (N_LAYERS_B, d), 0.02),
        "router_w": nrm(ks[19], (d, N_EXPERTS), d ** -0.5),
        "router_bias": nrm(ks[20], (N_EXPERTS,), 0.01),
        "moe_w_gate": nrm(ks[21], (DEPTH, N_EXPERTS, d, D_FF_EXPERT), d ** -0.5),
        "moe_w_up": nrm(ks[22], (DEPTH, N_EXPERTS, d, D_FF_EXPERT), d ** -0.5),
        "moe_w_down": nrm(ks[23], (DEPTH, N_EXPERTS, D_FF_EXPERT, d), D_FF_EXPERT ** -0.5),
        "final_norm_g": 1.0 + nrm(ks[24], (d,), 0.02),
    }


def reference(x, c, ada_w, ada_b, norm_g, gdn_w_in, gdn_conv_w, gdn_a_log, gdn_dt_bias,
              gdn_norm_g, gdn_w_out, conf_w_in, conf_b_in, conf_dw_w, conf_dw_b, conf_ln_g,
              conf_ln_b, conf_w_out, conf_b_out, router_w, router_bias, moe_w_gate, moe_w_up,
              moe_w_down, final_norm_g):
    c_act = jax.nn.silu(c)
    for i in range(DEPTH):
        mod = (c_act @ ada_w[i] + ada_b[i])[:, None, :]
        sh1, sc1, gt1, sh2, sc2, gt2 = jnp.split(mod, N_MOD, axis=-1)
        h = _rms_norm(x, norm_g[i, 0]) * (1.0 + sc1) + sh1
        j = i // N_MIXERS
        if i % N_MIXERS == 0:
            y = _gated_deltanet(h, gdn_w_in[j], gdn_conv_w[j], gdn_a_log[j], gdn_dt_bias[j],
                                gdn_norm_g[j], gdn_w_out[j])
        else:
            y = _conformer_conv(h, conf_w_in[j], conf_b_in[j], conf_dw_w[j], conf_dw_b[j],
                                conf_ln_g[j], conf_ln_b[j], conf_w_out[j], conf_b_out[j])
        x = x + gt1 * y
        h = _rms_norm(x, norm_g[i, 1]) * (1.0 + sc2) + sh2
        x = x + gt2 * _grouped_moe(h, router_w, router_bias, moe_w_gate[i], moe_w_up[i],
                                   moe_w_down[i])
    return _rms_norm(x, final_norm_g)
```

```python
import functools

import jax
import jax.numpy as jnp
from jax import lax
from jax.experimental import pallas as pl
from jax.experimental.pallas import tpu as pltpu

F32 = jnp.float32
BF16 = jnp.bfloat16

_EPS = 1e-6
_CHUNK = 64
_N_GROUPS = 4
_N_MOD = 6
_LANES = 128
_VMEM_LIMIT = 56 * 1024 * 1024


def _pick(dim, pref):
    t = min(pref, dim)
    while dim % t:
        t //= 2
    return max(t, 1)


def _params(sem, vmem=_VMEM_LIMIT):
    return pltpu.CompilerParams(dimension_semantics=sem, vmem_limit_bytes=vmem)


def _sigmoid(x):
    return 1.0 / (1.0 + jnp.exp(-x))


def _dot(a, b):
    return jnp.dot(a, b, preferred_element_type=F32)


def _dot_nt(a, b):
    return lax.dot_general(a, b, (((1,), (1,)), ((), ())), preferred_element_type=F32)


def _dot_tn(a, b):
    return lax.dot_general(a, b, (((0,), (0,)), ((), ())), preferred_element_type=F32)


def _adaln_kernel(c_ref, w_ref, b_ref, o_ref):
    c = c_ref[...]
    ca = (c * _sigmoid(c)).astype(BF16)
    o_ref[...] = _dot(ca, w_ref[...].astype(BF16)) + b_ref[...]


def _adaln(c, ada_w, ada_b):
    depth, d, n = ada_w.shape
    bsz = c.shape[0]
    bp = -(-bsz // 8) * 8
    c8 = jnp.pad(c, ((0, bp - bsz), (0, 0)))
    tn = _pick(n, 512)
    out = pl.pallas_call(
        _adaln_kernel,
        grid=(depth, n // tn),
        in_specs=[pl.BlockSpec((bp, d), lambda l, j: (0, 0)),
                  pl.BlockSpec((None, d, tn), lambda l, j: (l, 0, j)),
                  pl.BlockSpec((None, 1, tn), lambda l, j: (l, 0, j))],
        out_specs=pl.BlockSpec((None, bp, tn), lambda l, j: (l, 0, j)),
        out_shape=jax.ShapeDtypeStruct((depth, bp, n), F32),
        compiler_params=_params(("arbitrary", "arbitrary")),
        name="adaln",
    )(c8, ada_w, ada_b.reshape(depth, 1, n))
    return out[:, :bsz].reshape(depth, bsz, _N_MOD, 1, d)


def _norm_mod(x, g, sc, sh):
    ms = jnp.mean(x * x, axis=-1, keepdims=True)
    return x * lax.rsqrt(ms + _EPS) * g * (1.0 + sc) + sh


def _norm_mod_kernel(x_ref, g_ref, sc_ref, sh_ref, o_ref):
    o_ref[...] = _norm_mod(x_ref[...], g_ref[...], sc_ref[...], sh_ref[...]).astype(o_ref.dtype)


def _norm_mod_call(x, g, modl, sc_idx, sh_idx, seq, out_dtype):
    t, d = x.shape
    tm = _pick(seq, 256)
    tpb = seq // tm
    return pl.pallas_call(
        _norm_mod_kernel,
        grid=(t // tm,),
        in_specs=[pl.BlockSpec((tm, d), lambda i: (i, 0)),
                  pl.BlockSpec((1, d), lambda i: (0, 0)),
                  pl.BlockSpec((None, None, 1, d), lambda i: (i // tpb, sc_idx, 0, 0)),
                  pl.BlockSpec((None, None, 1, d), lambda i: (i // tpb, sh_idx, 0, 0))],
        out_specs=pl.BlockSpec((tm, d), lambda i: (i, 0)),
        out_shape=jax.ShapeDtypeStruct((t, d), out_dtype),
        compiler_params=_params(("arbitrary",)),
        name="norm_mod",
    )(x, g.reshape(1, d), modl, modl)


def _final_norm_kernel(x_ref, g_ref, o_ref):
    x = x_ref[...]
    ms = jnp.mean(x * x, axis=-1, keepdims=True)
    o_ref[...] = x * lax.rsqrt(ms + _EPS) * g_ref[...]


def _final_norm_call(x, g):
    t, d = x.shape
    tm = _pick(t, 256)
    return pl.pallas_call(
        _final_norm_kernel,
        grid=(t // tm,),
        in_specs=[pl.BlockSpec((tm, d), lambda i: (i, 0)),
                  pl.BlockSpec((1, d), lambda i: (0, 0))],
        out_specs=pl.BlockSpec((tm, d), lambda i: (i, 0)),
        out_shape=jax.ShapeDtypeStruct((t, d), F32),
        compiler_params=_params(("arbitrary",)),
        name="final_norm",
    )(x, g.reshape(1, d))


def _top2_sum(v):
    a = jnp.maximum(v[0], v[1]); b = jnp.minimum(v[0], v[1])
    c = jnp.maximum(v[2], v[3]); d = jnp.minimum(v[2], v[3])
    return jnp.maximum(a, c) + jnp.maximum(jnp.minimum(a, c), jnp.maximum(b, d))


def _first_argmax(vals):
    best = vals[0]
    idx = jnp.zeros(best.shape, jnp.int32)
    for e in range(1, len(vals)):
        take = vals[e] > best
        idx = jnp.where(take, e, idx)
        best = jnp.where(take, vals[e], best)
    return idx, best


def _moe_norm_router_kernel(x_ref, g_ref, sc_ref, sh_ref, rw_ref, rb_ref,
                            h_ref, idx_ref, wt_ref, *, n_exp):
    h = _norm_mod(x_ref[...], g_ref[...], sc_ref[...], sh_ref[...])
    h_ref[...] = h
    logits = jnp.dot(h, rw_ref[...], preferred_element_type=F32,
                     precision=lax.Precision.HIGHEST)
    lt = logits.T
    per_group = n_exp // _N_GROUPS
    aff = [_sigmoid(lt[e:e + 1, :]) for e in range(n_exp)]
    biased = [aff[e] + rb_ref[e] for e in range(n_exp)]
    gscore = [_top2_sum(biased[gi * per_group:(gi + 1) * per_group]) for gi in range(_N_GROUPS)]
    best_group, _ = _first_argmax(gscore)
    neg = jnp.full(aff[0].shape, -jnp.inf, F32)
    masked = [jnp.where(best_group == (e // per_group), biased[e], neg) for e in range(n_exp)]
    i1, _ = _first_argmax(masked)
    masked2 = [jnp.where(i1 == e, neg, masked[e]) for e in range(n_exp)]
    i2, _ = _first_argmax(masked2)
    zero = jnp.zeros(aff[0].shape, F32)
    a1 = zero
    a2 = zero
    for e in range(n_exp):
        a1 = a1 + jnp.where(i1 == e, aff[e], zero)
        a2 = a2 + jnp.where(i2 == e, aff[e], zero)
    den = a1 + a2
    idx_ref[...] = jnp.zeros(idx_ref.shape, jnp.int32)
    wt_ref[...] = jnp.zeros(wt_ref.shape, F32)
    idx_ref[0:1, :] = i1
    idx_ref[1:2, :] = i2
    wt_ref[0:1, :] = a1 / den
    wt_ref[1:2, :] = a2 / den


def _moe_norm_router_call(x, g, modl, seq, router_w, router_bias):
    t, d = x.shape
    n_exp = router_w.shape[1]
    assert n_exp % _N_GROUPS == 0 and n_exp // _N_GROUPS == 4 and n_exp <= _LANES
    tm = _pick(seq, 256)
    tpb = seq // tm
    rw = jnp.pad(router_w.astype(F32), ((0, 0), (0, _LANES - n_exp)))
    return pl.pallas_call(
        functools.partial(_moe_norm_router_kernel, n_exp=n_exp),
        grid=(t // tm,),
        in_specs=[pl.BlockSpec((tm, d), lambda i: (i, 0)),
                  pl.BlockSpec((1, d), lambda i: (0, 0)),
                  pl.BlockSpec((None, None, 1, d), lambda i: (i // tpb, 4, 0, 0)),
                  pl.BlockSpec((None, None, 1, d), lambda i: (i // tpb, 3, 0, 0)),
                  pl.BlockSpec((d, _LANES), lambda i: (0, 0)),
                  pl.BlockSpec(memory_space=pltpu.SMEM)],
        out_specs=[pl.BlockSpec((tm, d), lambda i: (i, 0)),
                   pl.BlockSpec((8, tm), lambda i: (0, i)),
                   pl.BlockSpec((8, tm), lambda i: (0, i))],
        out_shape=[jax.ShapeDtypeStruct((t, d), F32),
                   jax.ShapeDtypeStruct((8, t), jnp.int32),
                   jax.ShapeDtypeStruct((8, t), F32)],
        compiler_params=_params(("arbitrary",)),
        name="moe_norm_router",
    )(x, g.reshape(1, d), modl, modl, rw, router_bias.astype(F32))


def _mm_plain_kernel(x_ref, w_ref, o_ref, wb_ref):
    @pl.when(pl.program_id(1) == 0)
    def _():
        wb_ref[...] = w_ref[...].astype(BF16)
    o_ref[...] = _dot(x_ref[...], wb_ref[...]).astype(o_ref.dtype)


def _mm_plain(x, w, n_out, out_dtype, tm_pref=512, tn_pref=512):
    m, k = x.shape
    tm = _pick(m, tm_pref)
    tn = _pick(n_out, tn_pref)
    return pl.pallas_call(
        _mm_plain_kernel,
        grid=(n_out // tn, m // tm),
        in_specs=[pl.BlockSpec((tm, k), lambda j, i: (i, 0)),
                  pl.BlockSpec((k, tn), lambda j, i: (0, j))],
        out_specs=pl.BlockSpec((tm, tn), lambda j, i: (i, j)),
        out_shape=jax.ShapeDtypeStruct((m, n_out), out_dtype),
        scratch_shapes=[pltpu.VMEM((k, tn), BF16)],
        compiler_params=_params(("arbitrary", "arbitrary")),
        name="mm_plain",
    )(x, w)


def _mm_resid_kernel(x_ref, w_ref, b_ref, res_ref, gt_ref, o_ref, wb_ref):
    @pl.when(pl.program_id(1) == 0)
    def _():
        wb_ref[...] = w_ref[...].astype(BF16)
    y = _dot(x_ref[...], wb_ref[...]) + b_ref[...]
    o_ref[...] = res_ref[...] + gt_ref[...] * y


def _mm_resid(x, w, bias, res, modl, gate_idx, seq, tm_pref=512, tn_pref=512):
    m, k = x.shape
    n = w.shape[1]
    tm = _pick(seq, tm_pref)
    tn = _pick(n, tn_pref)
    tpb = seq // tm
    return pl.pallas_call(
        _mm_resid_kernel,
        grid=(n // tn, m // tm),
        in_specs=[pl.BlockSpec((tm, k), lambda j, i: (i, 0)),
                  pl.BlockSpec((k, tn), lambda j, i: (0, j)),
                  pl.BlockSpec((1, tn), lambda j, i: (0, j)),
                  pl.BlockSpec((tm, tn), lambda j, i: (i, j)),
                  pl.BlockSpec((None, None, 1, tn), lambda j, i: (i // tpb, gate_idx, 0, j))],
        out_specs=pl.BlockSpec((tm, tn), lambda j, i: (i, j)),
        out_shape=jax.ShapeDtypeStruct((m, n), F32),
        scratch_shapes=[pltpu.VMEM((k, tn), BF16)],
        compiler_params=_params(("arbitrary", "arbitrary")),
        name="mm_resid",
    )(x, w, bias.reshape(1, n), res, modl)


def _mm_glu_kernel(x_ref, wv_ref, wg_ref, bv_ref, bg_ref, o_ref, wvb_ref, wgb_ref):
    @pl.when(pl.program_id(1) == 0)
    def _():
        wvb_ref[...] = wv_ref[...].astype(BF16)
        wgb_ref[...] = wg_ref[...].astype(BF16)
    x = x_ref[...]
    val = _dot(x, wvb_ref[...]) + bv_ref[...]
    gate = _dot(x, wgb_ref[...]) + bg_ref[...]
    o_ref[...] = (val * _sigmoid(gate)).astype(o_ref.dtype)


def _mm_glu(x, w, b, out_dtype, tm_pref=512, tn_pref=256):
    m, k = x.shape
    n = w.shape[1] // 2
    tm = _pick(m, tm_pref)
    tn = _pick(n, tn_pref)
    nb = n // tn
    b2 = b.reshape(1, 2 * n)
    return pl.pallas_call(
        _mm_glu_kernel,
        grid=(nb, m // tm),
        in_specs=[pl.BlockSpec((tm, k), lambda j, i: (i, 0)),
                  pl.BlockSpec((k, tn), lambda j, i: (0, j)),
                  pl.BlockSpec((k, tn), lambda j, i: (0, j + nb)),
                  pl.BlockSpec((1, tn), lambda j, i: (0, j)),
                  pl.BlockSpec((1, tn), lambda j, i: (0, j + nb))],
        out_specs=pl.BlockSpec((tm, tn), lambda j, i: (i, j)),
        out_shape=jax.ShapeDtypeStruct((m, n), out_dtype),
        scratch_shapes=[pltpu.VMEM((k, tn), BF16), pltpu.VMEM((k, tn), BF16)],
        compiler_params=_params(("arbitrary", "arbitrary")),
        name="mm_glu",
    )(x, w, w, b2, b2)


def _gdn_ba_kernel(h_ref, w_ref, al_ref, dt_ref, o_ref, ot_ref, *, hv):
    p = _dot(h_ref[...], w_ref[...].astype(BF16))
    tm = p.shape[0]
    lane = lax.broadcasted_iota(jnp.int32, p.shape, 1)
    beta = _sigmoid(p)
    a = p + dt_ref[...]
    softplus = jnp.maximum(a, 0.0) + jnp.log1p(jnp.exp(-jnp.abs(a)))
    g = -jnp.exp(al_ref[...]) * softplus
    ri = lax.broadcasted_iota(jnp.int32, (_CHUNK, _CHUNK), 0)
    ci = lax.broadcasted_iota(jnp.int32, (_CHUNK, _CHUNK), 1)
    tril = (ri >= ci).astype(F32)
    parts = [jnp.dot(tril, g[c * _CHUNK:(c + 1) * _CHUNK], preferred_element_type=F32,
                     precision=lax.Precision.HIGHEST) for c in range(tm // _CHUNK)]
    gcum = jnp.concatenate(parts, axis=0)
    out = jnp.where(lane < hv, beta, jnp.where(lane < 2 * hv, gcum, 0.0))
    o_ref[...] = out
    ot_ref[...] = out.T


def _gdn_ba_call(h, w_ba, a_log, dt_bias, hv):
    t, d = h.shape
    assert 2 * hv <= _LANES
    pad = _LANES - 2 * hv
    w128 = jnp.pad(w_ba, ((0, 0), (0, pad)))
    al = jnp.pad(a_log.astype(F32), (hv, pad)).reshape(1, _LANES)
    dt = jnp.pad(dt_bias.astype(F32), (hv, pad)).reshape(1, _LANES)
    tm = _pick(t, 512)
    assert tm % _CHUNK == 0
    return pl.pallas_call(
        functools.partial(_gdn_ba_kernel, hv=hv),
        grid=(t // tm,),
        in_specs=[pl.BlockSpec((tm, d), lambda i: (i, 0)),
                  pl.BlockSpec((d, _LANES), lambda i: (0, 0)),
                  pl.BlockSpec((1, _LANES), lambda i: (0, 0)),
                  pl.BlockSpec((1, _LANES), lambda i: (0, 0))],
        out_specs=[pl.BlockSpec((tm, _LANES), lambda i: (i, 0)),
                   pl.BlockSpec((_LANES, tm), lambda i: (0, i))],
        out_shape=[jax.ShapeDtypeStruct((t, _LANES), F32),
                   jax.ShapeDtypeStruct((_LANES, t), F32)],
        compiler_params=_params(("arbitrary",)),
        name="gdn_ba",
    )(h, w128, al, dt)


def _conv_halo_rows(width):
    return 16 * (-(-(width - 1) // 16))


def _fill_conv_buffer(buf_ref, cur_ref, halo_ref, halo, tpb):
    ts = cur_ref.shape[0]
    first = (pl.program_id(0) % tpb) == 0
    buf_ref[0:halo, :] = jnp.where(first, 0.0, halo_ref[...].astype(F32))
    buf_ref[halo:halo + ts, :] = cur_ref[...].astype(F32)


def _dwconv_silu_kernel(cur_ref, halo_ref, w_ref, o_ref, buf_ref, *, width, halo, tpb, rows):
    ts = cur_ref.shape[0]
    _fill_conv_buffer(buf_ref, cur_ref, halo_ref, halo, tpb)
    base = halo - (width - 1)
    for r0 in range(0, ts, rows):
        acc = None
        for kk in range(width):
            term = buf_ref[base + r0 + kk:base + r0 + kk + rows, :] * w_ref[kk:kk + 1, :]
            acc = term if acc is None else acc + term
        o_ref[r0:r0 + rows, :] = (acc * _sigmoid(acc)).astype(o_ref.dtype)


def _dwconv_silu_call(x, n_ch, w, seq, ts_pref=512, tc_pref=512, rows=64):
    t = x.shape[0]
    width = w.shape[0]
    halo = _conv_halo_rows(width)
    ts = _pick(seq, ts_pref)
    tc = _pick(n_ch, tc_pref)
    rows = min(rows, ts)
    assert ts % halo == 0 and ts % rows == 0
    tpb = seq // ts
    hb = ts // halo
    return pl.pallas_call(
        functools.partial(_dwconv_silu_kernel, width=width, halo=halo, tpb=tpb, rows=rows),
        grid=(t // ts, n_ch // tc),
        in_specs=[pl.BlockSpec((ts, tc), lambda i, j: (i, j)),
                  pl.BlockSpec((halo, tc), lambda i, j: (jnp.maximum(i * hb - 1, 0), j)),
                  pl.BlockSpec((width, tc), lambda i, j: (0, j))],
        out_specs=pl.BlockSpec((ts, tc), lambda i, j: (i, j)),
        out_shape=jax.ShapeDtypeStruct((t, n_ch), BF16),
        scratch_shapes=[pltpu.VMEM((halo + ts, tc), F32)],
        compiler_params=_params(("arbitrary", "arbitrary")),
        name="dwconv_silu",
    )(x, x, w.astype(F32))


def _dwconv_ln_kernel(cur_ref, halo_ref, w_ref, b_ref, lg_ref, lb_ref, o_ref, buf_ref, acc_ref,
                      *, width, halo, tpb, rows, cw):
    ts, nch = cur_ref.shape
    _fill_conv_buffer(buf_ref, cur_ref, halo_ref, halo, tpb)
    base = halo - (width - 1)
    n_cc = nch // cw

    def conv_block(it, carry):
        r0 = pl.multiple_of((it // n_cc) * rows, rows)
        c0 = pl.multiple_of((it % n_cc) * cw, cw)
        blk = buf_ref[pl.ds(r0, rows + halo), pl.ds(c0, cw)]
        acc = None
        for kk in range(width):
            term = blk[base + kk:base + kk + rows] * w_ref[kk:kk + 1, pl.ds(c0, cw)]
            acc = term if acc is None else acc + term
        acc_ref[pl.ds(r0, rows), pl.ds(c0, cw)] = acc + b_ref[:, pl.ds(c0, cw)]
        return carry

    lax.fori_loop(0, (ts // rows) * n_cc, conv_block, 0)

    ln_rows = 16

    def norm_block(it, carry):
        r0 = pl.multiple_of(it * ln_rows, ln_rows)
        a = acc_ref[pl.ds(r0, ln_rows), :]
        mu = jnp.mean(a, axis=-1, keepdims=True)
        cen = a - mu
        var = jnp.mean(cen * cen, axis=-1, keepdims=True)
        y = cen * lax.rsqrt(var + _EPS) * lg_ref[...] + lb_ref[...]
        o_ref[pl.ds(r0, ln_rows), :] = (y * _sigmoid(y)).astype(o_ref.dtype)
        return carry

    lax.fori_loop(0, ts // ln_rows, norm_block, 0)


def _dwconv_ln_call(x, w, bias, ln_g, ln_b, seq, ts_pref=128, rows=32, cw=256):
    t, n_ch = x.shape
    width = w.shape[0]
    halo = _conv_halo_rows(width)
    ts = _pick(seq, ts_pref)
    cw = _pick(n_ch, cw)
    rows = min(rows, ts)
    assert ts % halo == 0 and ts % rows == 0 and ts % 16 == 0
    tpb = seq // ts
    hb = ts // halo

    def row(v):
        return v.astype(F32).reshape(1, n_ch)

    return pl.pallas_call(
        functools.partial(_dwconv_ln_kernel, width=width, halo=halo, tpb=tpb, rows=rows, cw=cw),
        grid=(t // ts,),
        in_specs=[pl.BlockSpec((ts, n_ch), lambda i: (i, 0)),
                  pl.BlockSpec((halo, n_ch), lambda i: (jnp.maximum(i * hb - 1, 0), 0)),
                  pl.BlockSpec((width, n_ch), lambda i: (0, 0)),
                  pl.BlockSpec((1, n_ch), lambda i: (0, 0)),
                  pl.BlockSpec((1, n_ch), lambda i: (0, 0)),
                  pl.BlockSpec((1, n_ch), lambda i: (0, 0))],
        out_specs=pl.BlockSpec((ts, n_ch), lambda i: (i, 0)),
        out_shape=jax.ShapeDtypeStruct((t, n_ch), BF16),
        scratch_shapes=[pltpu.VMEM((halo + ts, n_ch), F32), pltpu.VMEM((ts, n_ch), F32)],
        compiler_params=_params(("arbitrary",)),
        name="dwconv_ln",
    )(x, x, w.astype(F32), row(bias), row(ln_g), row(ln_b))


def _delta_kernel(q_ref, k_ref, v_ref, z_ref, bg_ref, grow_ref, ng_ref, o_ref, s_ref,
                  *, rep, dh, hv_total, scale):
    kh = pl.program_id(1)

    @pl.when(pl.program_id(2) == 0)
    def _():
        s_ref[...] = jnp.zeros(s_ref.shape, F32)

    ts = q_ref.shape[0]
    nc = ts // _CHUNK
    q = q_ref[...].astype(F32)
    k = k_ref[...].astype(F32)
    qn = q * lax.rsqrt(jnp.sum(q * q, axis=-1, keepdims=True) + _EPS) * scale
    kn = k * lax.rsqrt(jnp.sum(k * k, axis=-1, keepdims=True) + _EPS)
    qb = qn.astype(BF16)
    kb = kn.astype(BF16)
    bg = bg_ref[...]
    lane = lax.broadcasted_iota(jnp.int32, bg.shape, 1)
    ri = lax.broadcasted_iota(jnp.int32, (_CHUNK, _CHUNK), 0)
    ci = lax.broadcasted_iota(jnp.int32, (_CHUNK, _CHUNK), 1)
    causal = ri >= ci
    strict = ri > ci
    eye = (ri == ci).astype(F32)
    ng = ng_ref[...]
    n_sq = _CHUNK.bit_length() - 2

    kk_all = []
    qk_all = []
    for c in range(nc):
        sl = slice(c * _CHUNK, (c + 1) * _CHUNK)
        kk_all.append(_dot_nt(kb[sl], kb[sl]))
        qk_all.append(_dot_nt(qb[sl], kb[sl]))

    for r in range(rep):
        hv = kh * rep + r
        bcol = jnp.sum(jnp.where(lane == hv, bg, 0.0), axis=-1, keepdims=True)
        gcol = jnp.sum(jnp.where(lane == hv_total + hv, bg, 0.0), axis=-1, keepdims=True)
        grow = grow_ref[r]
        v = v_ref[:, r * dh:(r + 1) * dh].astype(F32)
        z = z_ref[:, r * dh:(r + 1) * dh].astype(F32)
        pre = []
        for c in range(nc):
            sl = slice(c * _CHUNK, (c + 1) * _CHUNK)
            gc = gcol[sl]
            bc = bcol[sl]
            gd = gc - grow[:, sl]
            decay = jnp.where(causal, jnp.exp(jnp.where(causal, gd, 0.0)), 0.0)
            x = -jnp.where(strict, kk_all[c] * bc * decay, 0.0)
            tinv = eye + x
            xp = x
            for _ in range(n_sq):
                xpb = xp.astype(BF16)
                xp = _dot(xpb, xpb)
                tinv = tinv + _dot(tinv.astype(BF16), xp.astype(BF16))
            eg = jnp.exp(gc)
            rhs = jnp.concatenate([v[sl] * bc, kn[sl] * (bc * eg)], axis=1).astype(BF16)
            sol = _dot(tinv.astype(BF16), rhs).astype(BF16)
            glast = gc[_CHUNK - 1:_CHUNK, :]
            k_dec = (kn[sl] * jnp.exp(glast - gc)).astype(BF16)
            kx = _dot_tn(k_dec, sol)
            qx = _dot((qk_all[c] * decay).astype(BF16), sol)
            q_eff = (qn[sl] * eg - qx[:, dh:]).astype(BF16)
            pre.append((q_eff, qx[:, :dh], kx[:, :dh], kx[:, dh:].astype(BF16), jnp.exp(glast)))
        state = s_ref[r]
        for c in range(nc):
            sl = slice(c * _CHUNK, (c + 1) * _CHUNK)
            q_eff, o_loc, ku, kw, gl = pre[c]
            sb = state.astype(BF16)
            o = _dot(q_eff, sb) + o_loc
            state = gl * state - _dot(kw, sb) + ku
            on = o * lax.rsqrt(jnp.mean(o * o, axis=-1, keepdims=True) + _EPS) * ng
            zc = z[sl]
            o_ref[sl, r * dh:(r + 1) * dh] = (on * (zc * _sigmoid(zc))).astype(o_ref.dtype)
        s_ref[r] = state


def _delta_call(qkv, proj, bg, bgt3, norm_g, bsz, seq, hk, hv, dh):
    t = qkv.shape[0]
    rep = hv // hk
    assert hv % hk == 0 and (2 * hk) % rep == 0 and dh % _LANES == 0
    conv_dim = qkv.shape[1]
    ts = _pick(seq, 256)
    assert ts % _CHUNK == 0
    ns = seq // ts
    v_blk0 = (2 * hk * dh) // (rep * dh)
    z_blk0 = conv_dim // (rep * dh)
    return pl.pallas_call(
        functools.partial(_delta_kernel, rep=rep, dh=dh, hv_total=hv, scale=float(dh) ** -0.5),
        grid=(bsz, hk, ns),
        in_specs=[pl.BlockSpec((ts, dh), lambda b, h, s: (b * ns + s, h)),
                  pl.BlockSpec((ts, dh), lambda b, h, s: (b * ns + s, hk + h)),
                  pl.BlockSpec((ts, rep * dh), lambda b, h, s: (b * ns + s, v_blk0 + h)),
                  pl.BlockSpec((ts, rep * dh), lambda b, h, s: (b * ns + s, z_blk0 + h)),
                  pl.BlockSpec((ts, _LANES), lambda b, h, s: (b * ns + s, 0)),
                  pl.BlockSpec((rep, 1, ts), lambda b, h, s: (hv // rep + h, 0, b * ns + s)),
                  pl.BlockSpec((1, dh), lambda b, h, s: (0, 0))],
        out_specs=pl.BlockSpec((ts, rep * dh), lambda b, h, s: (b * ns + s, h)),
        out_shape=jax.ShapeDtypeStruct((t, hv * dh), BF16),
        scratch_shapes=[pltpu.VMEM((rep, dh, dh), F32)],
        compiler_params=_params(("arbitrary", "arbitrary", "arbitrary")),
        name="gdn_delta",
    )(qkv, qkv, qkv, proj, bg, bgt3, norm_g.astype(F32).reshape(1, dh))


def _gather_rows_kernel(src_ref, x_hbm, o_hbm, sem, *, tm):
    base = pl.program_id(0) * tm

    def row_copy(r):
        return pltpu.make_async_copy(x_hbm.at[pl.ds(src_ref[base + r], 1), :],
                                     o_hbm.at[pl.ds(base + r, 1), :], sem)

    def issue(r, carry):
        row_copy(r).start()
        return carry

    def drain(r, carry):
        row_copy(r).wait()
        return carry

    lax.fori_loop(0, tm, issue, 0)
    lax.fori_loop(0, tm, drain, 0)


def _gather_rows_call(src, x, n_rows, tm):
    d = x.shape[1]
    return pl.pallas_call(
        functools.partial(_gather_rows_kernel, tm=tm),
        grid_spec=pltpu.PrefetchScalarGridSpec(
            num_scalar_prefetch=1,
            grid=(n_rows // tm,),
            in_specs=[pl.BlockSpec(memory_space=pl.ANY)],
            out_specs=pl.BlockSpec(memory_space=pl.ANY),
            scratch_shapes=[pltpu.SemaphoreType.DMA(())]),
        out_shape=jax.ShapeDtypeStruct((n_rows, d), x.dtype),
        compiler_params=_params(("arbitrary",)),
        name="moe_gather",
    )(src, x)


def _expert_changed(te_ref, t):
    return jnp.logical_or(t == 0, te_ref[t] != te_ref[jnp.maximum(t - 1, 0)])


def _moe_up_kernel(te_ref, nu_ref, x_ref, wg_ref, wu_ref, o_ref, wgb_ref, wub_ref):
    t = pl.program_id(1)

    @pl.when(_expert_changed(te_ref, t))
    def _():
        wgb_ref[...] = wg_ref[...].astype(BF16)
        wub_ref[...] = wu_ref[...].astype(BF16)

    @pl.when(t < nu_ref[0])
    def _():
        x = x_ref[...].astype(BF16)
        g = _dot(x, wgb_ref[...])
        u = _dot(x, wub_ref[...])
        o_ref[...] = (g * _sigmoid(g) * u).astype(o_ref.dtype)

    @pl.when(t >= nu_ref[0])
    def _():
        o_ref[...] = jnp.zeros(o_ref.shape, o_ref.dtype)


def _moe_up_call(te, nu, xs, w_gate, w_up, tm, tf_pref=512):
    n_rows, d = xs.shape
    f = w_gate.shape[2]
    tf = _pick(f, tf_pref)
    return pl.pallas_call(
        _moe_up_kernel,
        grid_spec=pltpu.PrefetchScalarGridSpec(
            num_scalar_prefetch=2,
            grid=(f // tf, n_rows // tm),
            in_specs=[pl.BlockSpec((tm, d), lambda j, t, te, nu: (t, 0)),
                      pl.BlockSpec((None, d, tf), lambda j, t, te, nu: (te[t], 0, j)),
                      pl.BlockSpec((None, d, tf), lambda j, t, te, nu: (te[t], 0, j))],
            out_specs=pl.BlockSpec((tm, tf), lambda j, t, te, nu: (t, j)),
            scratch_shapes=[pltpu.VMEM((d, tf), BF16), pltpu.VMEM((d, tf), BF16)]),
        out_shape=jax.ShapeDtypeStruct((n_rows, f), BF16),
        compiler_params=_params(("arbitrary", "arbitrary")),
        name="moe_up",
    )(te, nu, xs, w_gate, w_up)


def _moe_down_kernel(te_ref, nu_ref, h_ref, wd_ref, o_ref, wdb_ref):
    t = pl.program_id(1)

    @pl.when(_expert_changed(te_ref, t))
    def _():
        wdb_ref[...] = wd_ref[...].astype(BF16)

    @pl.when(t < nu_ref[0])
    def _():
        o_ref[...] = _dot(h_ref[...], wdb_ref[...])

    @pl.when(t >= nu_ref[0])
    def _():
        o_ref[...] = jnp.zeros(o_ref.shape, o_ref.dtype)


def _moe_down_call(te, nu, hmid, w_down, tm, tn_pref=2048):
    n_rows, f = hmid.shape
    d = w_down.shape[2]
    tn = _pick(d, tn_pref)
    return pl.pallas_call(
        _moe_down_kernel,
        grid_spec=pltpu.PrefetchScalarGridSpec(
            num_scalar_prefetch=2,
            grid=(d // tn, n_rows // tm),
            in_specs=[pl.BlockSpec((tm, f), lambda j, t, te, nu: (t, 0)),
                      pl.BlockSpec((None, f, tn), lambda j, t, te, nu: (te[t], 0, j))],
            out_specs=pl.BlockSpec((tm, tn), lambda j, t, te, nu: (t, j)),
            scratch_shapes=[pltpu.VMEM((f, tn), BF16)]),
        out_shape=jax.ShapeDtypeStruct((n_rows, d), F32),
        compiler_params=_params(("arbitrary", "arbitrary")),
        name="moe_down",
    )(te, nu, hmid, w_down)


def _moe_combine_kernel(pos_ref, x_ref, wt_ref, gt_ref, ys_hbm, o_ref, buf_ref, sem, *, tm):
    base = pl.program_id(0) * tm

    def row_copy(r, kk):
        return pltpu.make_async_copy(ys_hbm.at[pl.ds(pos_ref[(base + r) * 2 + kk], 1), :],
                                     buf_ref.at[kk, pl.ds(r, 1), :], sem)

    def issue(r, carry):
        row_copy(r, 0).start()
        row_copy(r, 1).start()
        return carry

    def drain(r, carry):
        row_copy(r, 0).wait()
        row_copy(r, 1).wait()
        return carry

    lax.fori_loop(0, tm, issue, 0)
    lax.fori_loop(0, tm, drain, 0)
    wt = wt_ref[...]
    y = wt[:, 0:1] * buf_ref[0] + wt[:, 1:2] * buf_ref[1]
    o_ref[...] = x_ref[...] + gt_ref[...] * y


def _moe_combine_call(pos, x, wt, modl, gate_idx, ys, seq):
    t, d = x.shape
    tm = _pick(seq, 128)
    tpb = seq // tm
    return pl.pallas_call(
        functools.partial(_moe_combine_kernel, tm=tm),
        grid_spec=pltpu.PrefetchScalarGridSpec(
            num_scalar_prefetch=1,
            grid=(t // tm,),
            in_specs=[pl.BlockSpec((tm, d), lambda i, pos: (i, 0)),
                      pl.BlockSpec((tm, 2), lambda i, pos: (i, 0)),
                      pl.BlockSpec((None, None, 1, d), lambda i, pos: (i // tpb, gate_idx, 0, 0)),
                      pl.BlockSpec(memory_space=pl.ANY)],
            out_specs=pl.BlockSpec((tm, d), lambda i, pos: (i, 0)),
            scratch_shapes=[pltpu.VMEM((2, tm, d), F32), pltpu.SemaphoreType.DMA(())]),
        out_shape=jax.ShapeDtypeStruct((t, d), F32),
        compiler_params=_params(("arbitrary",)),
        name="moe_combine",
    )(pos, x, wt, modl, ys)


def _route_tables(idx, n_exp, tm):
    n_sel = idx.shape[0] * idx.shape[1]
    e_flat = idx.reshape(-1)
    onehot = (e_flat[:, None] == jnp.arange(n_exp, dtype=jnp.int32)[None, :]).astype(jnp.int32)
    csum = jnp.cumsum(onehot, axis=0)
    counts = csum[-1]
    rank = jnp.sum(onehot * (csum - 1), axis=1)
    padded = ((counts + tm - 1) // tm) * tm
    ends = jnp.cumsum(padded)
    starts = ends - padded
    pos = jnp.sum(onehot * starts[None, :], axis=1) + rank
    n_rows = -(-n_sel // tm) * tm + n_exp * tm
    src = jnp.zeros((n_rows,), jnp.int32).at[pos].set(
        jnp.arange(n_sel, dtype=jnp.int32) // idx.shape[1])
    n_tiles = n_rows // tm
    n_used = (ends[-1] // tm).astype(jnp.int32)
    tile = jnp.minimum(jnp.arange(n_tiles, dtype=jnp.int32), n_used - 1)
    te = jnp.sum((tile[:, None] * tm >= ends[None, :]).astype(jnp.int32), axis=1)
    te = jnp.minimum(te, n_exp - 1).astype(jnp.int32)
    return pos.astype(jnp.int32), src, te, n_used.reshape(1), n_rows


def _moe_block(x, norm_g, modl, seq, router_w, router_bias, w_gate, w_up, w_down):
    n_exp = router_w.shape[1]
    h, idx_t, wt_t = _moe_norm_router_call(x, norm_g, modl, seq, router_w, router_bias)
    idx = idx_t[:2].T
    wt = wt_t[:2].T
    tm = 256
    pos, src, te, nu, n_rows = _route_tables(idx, n_exp, tm)
    xs = _gather_rows_call(src, h, n_rows, tm)
    hmid = _moe_up_call(te, nu, xs, w_gate, w_up, tm)
    ys = _moe_down_call(te, nu, hmid, w_down, tm)
    return _moe_combine_call(pos, x, wt, modl, 5, ys, seq)


def _gdn_mixer(x, norm_g, modl, bsz, seq, w_in, conv_w, a_log, dt_bias, gnorm_g, w_out):
    d = x.shape[1]
    hv = a_log.shape[0]
    dh = gnorm_g.shape[0]
    conv_dim = conv_w.shape[1]
    value_dim = hv * dh
    key_dim = (conv_dim - value_dim) // 2
    hk = key_dim // dh
    assert w_in.shape[1] == conv_dim + value_dim + 2 * hv
    h = _norm_mod_call(x, norm_g, modl, 1, 0, seq, BF16)
    proj = _mm_plain(h, w_in, conv_dim + value_dim, BF16)
    bg, bgt = _gdn_ba_call(h, w_in[:, conv_dim + value_dim:], a_log, dt_bias, hv)
    qkv = _dwconv_silu_call(proj, conv_dim, conv_w, seq)
    o = _delta_call(qkv, proj, bg, bgt.reshape(_LANES, 1, -1), gnorm_g, bsz, seq, hk, hv, dh)
    return _mm_resid(o, w_out, jnp.zeros((d,), F32), x, modl, 2, seq)


def _conformer_mixer(x, norm_g, modl, seq, w_in, b_in, dw_w, dw_b, ln_g, ln_b, w_out, b_out):
    inner = w_in.shape[1] // 2
    h = _norm_mod_call(x, norm_g, modl, 1, 0, seq, BF16)
    u = _mm_glu(h, w_in, b_in, BF16)
    u = _dwconv_ln_call(u, dw_w, dw_b, ln_g, ln_b, seq)
    return _mm_resid(u, w_out, b_out, x, modl, 2, seq)


def kernel(x, c, ada_w, ada_b, norm_g, gdn_w_in, gdn_conv_w, gdn_a_log, gdn_dt_bias, gdn_norm_g, gdn_w_out, conf_w_in, conf_b_in, conf_dw_w, conf_dw_b, conf_ln_g, conf_ln_b, conf_w_out, conf_b_out, router_w, router_bias, moe_w_gate, moe_w_up, moe_w_down, final_norm_g):
    bsz, seq, d = x.shape
    depth = ada_w.shape[0]
    mod = _adaln(c, ada_w, ada_b)
    xf = x.reshape(bsz * seq, d)
    for i in range(depth):
        j = i // 2
        if i % 2 == 0:
            xf = _gdn_mixer(xf, norm_g[i, 0], mod[i], bsz, seq, gdn_w_in[j], gdn_conv_w[j],
                            gdn_a_log[j], gdn_dt_bias[j], gdn_norm_g[j], gdn_w_out[j])
        else:
            xf = _conformer_mixer(xf, norm_g[i, 0], mod[i], seq, conf_w_in[j], conf_b_in[j],
                                  conf_dw_w[j], conf_dw_b[j], conf_ln_g[j], conf_ln_b[j],
                                  conf_w_out[j], conf_b_out[j])
        xf = _moe_block(xf, norm_g[i, 1], mod[i], seq, router_w, router_bias,
                        moe_w_gate[i], moe_w_up[i], moe_w_down[i])
    return _final_norm_call(xf, final_norm_g).reshape(bsz, seq, d)
```

```python
import functools

import jax
import jax.numpy as jnp
from jax import lax
from jax.experimental import pallas as pl
from jax.experimental.pallas import tpu as pltpu

F32 = jnp.float32
BF16 = jnp.bfloat16

_EPS = 1e-6
_CHUNK = 64
_N_GROUPS = 4
_N_MOD = 6
_LANES = 128
_VMEM_LIMIT = 56 * 1024 * 1024


def _pick(dim, pref):
    t = min(pref, dim)
    while dim % t:
        t //= 2
    return max(t, 1)


def _params(sem, vmem=_VMEM_LIMIT):
    return pltpu.CompilerParams(dimension_semantics=sem, vmem_limit_bytes=vmem)


def _sigmoid(x):
    return 1.0 / (1.0 + jnp.exp(-x))


def _dot(a, b):
    return jnp.dot(a, b, preferred_element_type=F32)


def _dot_nt(a, b):
    return lax.dot_general(a, b, (((1,), (1,)), ((), ())), preferred_element_type=F32)


def _dot_tn(a, b):
    return lax.dot_general(a, b, (((0,), (0,)), ((), ())), preferred_element_type=F32)


def _adaln_kernel(c_ref, w_ref, b_ref, o_ref):
    c = c_ref[...]
    ca = (c * _sigmoid(c)).astype(BF16)
    o_ref[...] = _dot(ca, w_ref[...].astype(BF16)) + b_ref[...]


def _adaln(c, ada_w, ada_b):
    depth, d, n = ada_w.shape
    bsz = c.shape[0]
    bp = -(-bsz // 8) * 8
    c8 = jnp.pad(c, ((0, bp - bsz), (0, 0)))
    tn = _pick(n, 512)
    out = pl.pallas_call(
        _adaln_kernel,
        grid=(depth, n // tn),
        in_specs=[pl.BlockSpec((bp, d), lambda l, j: (0, 0)),
                  pl.BlockSpec((None, d, tn), lambda l, j: (l, 0, j)),
                  pl.BlockSpec((None, 1, tn), lambda l, j: (l, 0, j))],
        out_specs=pl.BlockSpec((None, bp, tn), lambda l, j: (l, 0, j)),
        out_shape=jax.ShapeDtypeStruct((depth, bp, n), F32),
        compiler_params=_params(("arbitrary", "arbitrary")),
        name="adaln",
    )(c8, ada_w, ada_b.reshape(depth, 1, n))
    return out[:, :bsz].reshape(depth, bsz, _N_MOD, 1, d)


def _norm_mod(x, g, sc, sh):
    ms = jnp.mean(x * x, axis=-1, keepdims=True)
    return x * lax.rsqrt(ms + _EPS) * g * (1.0 + sc) + sh


def _norm_mod_kernel(x_ref, g_ref, sc_ref, sh_ref, o_ref):
    o_ref[...] = _norm_mod(x_ref[...], g_ref[...], sc_ref[...], sh_ref[...]).astype(o_ref.dtype)


def _norm_mod_call(x, g, modl, sc_idx, sh_idx, seq, out_dtype):
    t, d = x.shape
    tm = _pick(seq, 256)
    tpb = seq // tm
    return pl.pallas_call(
        _norm_mod_kernel,
        grid=(t // tm,),
        in_specs=[pl.BlockSpec((tm, d), lambda i: (i, 0)),
                  pl.BlockSpec((1, d), lambda i: (0, 0)),
                  pl.BlockSpec((None, None, 1, d), lambda i: (i // tpb, sc_idx, 0, 0)),
                  pl.BlockSpec((None, None, 1, d), lambda i: (i // tpb, sh_idx, 0, 0))],
        out_specs=pl.BlockSpec((tm, d), lambda i: (i, 0)),
        out_shape=jax.ShapeDtypeStruct((t, d), out_dtype),
        compiler_params=_params(("arbitrary",)),
        name="norm_mod",
    )(x, g.reshape(1, d), modl, modl)


def _final_norm_kernel(x_ref, g_ref, o_ref):
    x = x_ref[...]
    ms = jnp.mean(x * x, axis=-1, keepdims=True)
    o_ref[...] = x * lax.rsqrt(ms + _EPS) * g_ref[...]


def _final_norm_call(x, g):
    t, d = x.shape
    tm = _pick(t, 256)
    return pl.pallas_call(
        _final_norm_kernel,
        grid=(t // tm,),
        in_specs=[pl.BlockSpec((tm, d), lambda i: (i, 0)),
                  pl.BlockSpec((1, d), lambda i: (0, 0))],
        out_specs=pl.BlockSpec((tm, d), lambda i: (i, 0)),
        out_shape=jax.ShapeDtypeStruct((t, d), F32),
        compiler_params=_params(("arbitrary",)),
        name="final_norm",
    )(x, g.reshape(1, d))


def _top2_sum(v):
    a = jnp.maximum(v[0], v[1]); b = jnp.minimum(v[0], v[1])
    c = jnp.maximum(v[2], v[3]); d = jnp.minimum(v[2], v[3])
    return jnp.maximum(a, c) + jnp.maximum(jnp.minimum(a, c), jnp.maximum(b, d))


def _first_argmax(vals):
    best = vals[0]
    idx = jnp.zeros(best.shape, jnp.int32)
    for e in range(1, len(vals)):
        take = vals[e] > best
        idx = jnp.where(take, e, idx)
        best = jnp.where(take, vals[e], best)
    return idx, best


def _moe_norm_router_kernel(x_ref, g_ref, sc_ref, sh_ref, rw_ref, rb_ref,
                            h_ref, idx_ref, wt_ref, *, n_exp):
    h = _norm_mod(x_ref[...], g_ref[...], sc_ref[...], sh_ref[...])
    h_ref[...] = h
    logits = jnp.dot(h, rw_ref[...], preferred_element_type=F32,
                     precision=lax.Precision.HIGHEST)
    lt = logits.T
    per_group = n_exp // _N_GROUPS
    aff = [_sigmoid(lt[e:e + 1, :]) for e in range(n_exp)]
    biased = [aff[e] + rb_ref[e] for e in range(n_exp)]
    gscore = [_top2_sum(biased[gi * per_group:(gi + 1) * per_group]) for gi in range(_N_GROUPS)]
    best_group, _ = _first_argmax(gscore)
    neg = jnp.full(aff[0].shape, -jnp.inf, F32)
    masked = [jnp.where(best_group == (e // per_group), biased[e], neg) for e in range(n_exp)]
    i1, _ = _first_argmax(masked)
    masked2 = [jnp.where(i1 == e, neg, masked[e]) for e in range(n_exp)]
    i2, _ = _first_argmax(masked2)
    zero = jnp.zeros(aff[0].shape, F32)
    a1 = zero
    a2 = zero
    for e in range(n_exp):
        a1 = a1 + jnp.where(i1 == e, aff[e], zero)
        a2 = a2 + jnp.where(i2 == e, aff[e], zero)
    den = a1 + a2
    idx_ref[...] = jnp.zeros(idx_ref.shape, jnp.int32)
    wt_ref[...] = jnp.zeros(wt_ref.shape, F32)
    idx_ref[0:1, :] = i1
    idx_ref[1:2, :] = i2
    wt_ref[0:1, :] = a1 / den
    wt_ref[1:2, :] = a2 / den


def _moe_norm_router_call(x, g, modl, seq, router_w, router_bias):
    t, d = x.shape
    n_exp = router_w.shape[1]
    assert n_exp % _N_GROUPS == 0 and n_exp // _N_GROUPS == 4 and n_exp <= _LANES
    tm = _pick(seq, 256)
    tpb = seq // tm
    rw = jnp.pad(router_w.astype(F32), ((0, 0), (0, _LANES - n_exp)))
    return pl.pallas_call(
        functools.partial(_moe_norm_router_kernel, n_exp=n_exp),
        grid=(t // tm,),
        in_specs=[pl.BlockSpec((tm, d), lambda i: (i, 0)),
                  pl.BlockSpec((1, d), lambda i: (0, 0)),
                  pl.BlockSpec((None, None, 1, d), lambda i: (i // tpb, 4, 0, 0)),
                  pl.BlockSpec((None, None, 1, d), lambda i: (i // tpb, 3, 0, 0)),
                  pl.BlockSpec((d, _LANES), lambda i: (0, 0)),
                  pl.BlockSpec(memory_space=pltpu.SMEM)],
        out_specs=[pl.BlockSpec((tm, d), lambda i: (i, 0)),
                   pl.BlockSpec((8, tm), lambda i: (0, i)),
                   pl.BlockSpec((8, tm), lambda i: (0, i))],
        out_shape=[jax.ShapeDtypeStruct((t, d), F32),
                   jax.ShapeDtypeStruct((8, t), jnp.int32),
                   jax.ShapeDtypeStruct((8, t), F32)],
        compiler_params=_params(("arbitrary",)),
        name="moe_norm_router",
    )(x, g.reshape(1, d), modl, modl, rw, router_bias.astype(F32))


def _mm_plain_kernel(x_ref, w_ref, o_ref, wb_ref):
    @pl.when(pl.program_id(1) == 0)
    def _():
        wb_ref[...] = w_ref[...].astype(BF16)
    o_ref[...] = _dot(x_ref[...], wb_ref[...]).astype(o_ref.dtype)


def _mm_plain(x, w, n_out, out_dtype, tm_pref=512, tn_pref=512):
    m, k = x.shape
    tm = _pick(m, tm_pref)
    tn = _pick(n_out, tn_pref)
    return pl.pallas_call(
        _mm_plain_kernel,
        grid=(n_out // tn, m // tm),
        in_specs=[pl.BlockSpec((tm, k), lambda j, i: (i, 0)),
                  pl.BlockSpec((k, tn), lambda j, i: (0, j))],
        out_specs=pl.BlockSpec((tm, tn), lambda j, i: (i, j)),
        out_shape=jax.ShapeDtypeStruct((m, n_out), out_dtype),
        scratch_shapes=[pltpu.VMEM((k, tn), BF16)],
        compiler_params=_params(("arbitrary", "arbitrary")),
        name="mm_plain",
    )(x, w)


def _mm_resid_kernel(x_ref, w_ref, b_ref, res_ref, gt_ref, o_ref, wb_ref):
    @pl.when(pl.program_id(1) == 0)
    def _():
        wb_ref[...] = w_ref[...].astype(BF16)
    y = _dot(x_ref[...], wb_ref[...]) + b_ref[...]
    o_ref[...] = res_ref[...] + gt_ref[...] * y


def _mm_resid(x, w, bias, res, modl, gate_idx, seq, tm_pref=512, tn_pref=512):
    m, k = x.shape
    n = w.shape[1]
    tm = _pick(seq, tm_pref)
    tn = _pick(n, tn_pref)
    tpb = seq // tm
    return pl.pallas_call(
        _mm_resid_kernel,
        grid=(n // tn, m // tm),
        in_specs=[pl.BlockSpec((tm, k), lambda j, i: (i, 0)),
                  pl.BlockSpec((k, tn), lambda j, i: (0, j)),
                  pl.BlockSpec((1, tn), lambda j, i: (0, j)),
                  pl.BlockSpec((tm, tn), lambda j, i: (i, j)),
                  pl.BlockSpec((None, None, 1, tn), lambda j, i: (i // tpb, gate_idx, 0, j))],
        out_specs=pl.BlockSpec((tm, tn), lambda j, i: (i, j)),
        out_shape=jax.ShapeDtypeStruct((m, n), F32),
        scratch_shapes=[pltpu.VMEM((k, tn), BF16)],
        compiler_params=_params(("arbitrary", "arbitrary")),
        name="mm_resid",
    )(x, w, bias.reshape(1, n), res, modl)


def _mm_glu_kernel(x_ref, wv_ref, wg_ref, bv_ref, bg_ref, o_ref, wvb_ref, wgb_ref):
    @pl.when(pl.program_id(1) == 0)
    def _():
        wvb_ref[...] = wv_ref[...].astype(BF16)
        wgb_ref[...] = wg_ref[...].astype(BF16)
    x = x_ref[...]
    val = _dot(x, wvb_ref[...]) + bv_ref[...]
    gate = _dot(x, wgb_ref[...]) + bg_ref[...]
    o_ref[...] = (val * _sigmoid(gate)).astype(o_ref.dtype)


def _mm_glu(x, w, b, out_dtype, tm_pref=512, tn_pref=256):
    m, k = x.shape
    n = w.shape[1] // 2
    tm = _pick(m, tm_pref)
    tn = _pick(n, tn_pref)
    nb = n // tn
    b2 = b.reshape(1, 2 * n)
    return pl.pallas_call(
        _mm_glu_kernel,
        grid=(nb, m // tm),
        in_specs=[pl.BlockSpec((tm, k), lambda j, i: (i, 0)),
                  pl.BlockSpec((k, tn), lambda j, i: (0, j)),
                  pl.BlockSpec((k, tn), lambda j, i: (0, j + nb)),
                  pl.BlockSpec((1, tn), lambda j, i: (0, j)),
                  pl.BlockSpec((1, tn), lambda j, i: (0, j + nb))],
        out_specs=pl.BlockSpec((tm, tn), lambda j, i: (i, j)),
        out_shape=jax.ShapeDtypeStruct((m, n), out_dtype),
        scratch_shapes=[pltpu.VMEM((k, tn), BF16), pltpu.VMEM((k, tn), BF16)],
        compiler_params=_params(("arbitrary", "arbitrary")),
        name="mm_glu",
    )(x, w, w, b2, b2)


def _gdn_ba_kernel(h_ref, w_ref, al_ref, dt_ref, o_ref, ot_ref, *, hv):
    p = _dot(h_ref[...], w_ref[...].astype(BF16))
    tm = p.shape[0]
    lane = lax.broadcasted_iota(jnp.int32, p.shape, 1)
    beta = _sigmoid(p)
    a = p + dt_ref[...]
    softplus = jnp.maximum(a, 0.0) + jnp.log1p(jnp.exp(-jnp.abs(a)))
    g = -jnp.exp(al_ref[...]) * softplus
    ri = lax.broadcasted_iota(jnp.int32, (_CHUNK, _CHUNK), 0)
    ci = lax.broadcasted_iota(jnp.int32, (_CHUNK, _CHUNK), 1)
    tril = (ri >= ci).astype(F32)
    parts = [jnp.dot(tril, g[c * _CHUNK:(c + 1) * _CHUNK], preferred_element_type=F32,
                     precision=lax.Precision.HIGHEST) for c in range(tm // _CHUNK)]
    gcum = jnp.concatenate(parts, axis=0)
    out = jnp.where(lane < hv, beta, jnp.where(lane < 2 * hv, gcum, 0.0))
    o_ref[...] = out
    ot_ref[...] = out.T


def _gdn_ba_call(h, w_ba, a_log, dt_bias, hv):
    t, d = h.shape
    assert 2 * hv <= _LANES
    pad = _LANES - 2 * hv
    w128 = jnp.pad(w_ba, ((0, 0), (0, pad)))
    al = jnp.pad(a_log.astype(F32), (hv, pad)).reshape(1, _LANES)
    dt = jnp.pad(dt_bias.astype(F32), (hv, pad)).reshape(1, _LANES)
    tm = _pick(t, 512)
    assert tm % _CHUNK == 0
    return pl.pallas_call(
        functools.partial(_gdn_ba_kernel, hv=hv),
        grid=(t // tm,),
        in_specs=[pl.BlockSpec((tm, d), lambda i: (i, 0)),
                  pl.BlockSpec((d, _LANES), lambda i: (0, 0)),
                  pl.BlockSpec((1, _LANES), lambda i: (0, 0)),
                  pl.BlockSpec((1, _LANES), lambda i: (0, 0))],
        out_specs=[pl.BlockSpec((tm, _LANES), lambda i: (i, 0)),
                   pl.BlockSpec((_LANES, tm), lambda i: (0, i))],
        out_shape=[jax.ShapeDtypeStruct((t, _LANES), F32),
                   jax.ShapeDtypeStruct((_LANES, t), F32)],
        compiler_params=_params(("arbitrary",)),
        name="gdn_ba",
    )(h, w128, al, dt)


def _conv_halo_rows(width):
    return 16 * (-(-(width - 1) // 16))


def _fill_conv_buffer(buf_ref, cur_ref, halo_ref, halo, tpb):
    ts = cur_ref.shape[0]
    first = (pl.program_id(0) % tpb) == 0
    buf_ref[0:halo, :] = jnp.where(first, 0.0, halo_ref[...].astype(F32))
    buf_ref[halo:halo + ts, :] = cur_ref[...].astype(F32)


def _dwconv_silu_kernel(cur_ref, halo_ref, w_ref, o_ref, buf_ref, *, width, halo, tpb, rows):
    ts = cur_ref.shape[0]
    _fill_conv_buffer(buf_ref, cur_ref, halo_ref, halo, tpb)
    base = halo - (width - 1)
    for r0 in range(0, ts, rows):
        acc = None
        for kk in range(width):
            term = buf_ref[base + r0 + kk:base + r0 + kk + rows, :] * w_ref[kk:kk + 1, :]
            acc = term if acc is None else acc + term
        o_ref[r0:r0 + rows, :] = (acc * _sigmoid(acc)).astype(o_ref.dtype)


def _dwconv_silu_call(x, n_ch, w, seq, ts_pref=512, tc_pref=512, rows=64):
    t = x.shape[0]
    width = w.shape[0]
    halo = _conv_halo_rows(width)
    ts = _pick(seq, ts_pref)
    tc = _pick(n_ch, tc_pref)
    rows = min(rows, ts)
    assert ts % halo == 0 and ts % rows == 0
    tpb = seq // ts
    hb = ts // halo
    return pl.pallas_call(
        functools.partial(_dwconv_silu_kernel, width=width, halo=halo, tpb=tpb, rows=rows),
        grid=(t // ts, n_ch // tc),
        in_specs=[pl.BlockSpec((ts, tc), lambda i, j: (i, j)),
                  pl.BlockSpec((halo, tc), lambda i, j: (jnp.maximum(i * hb - 1, 0), j)),
                  pl.BlockSpec((width, tc), lambda i, j: (0, j))],
        out_specs=pl.BlockSpec((ts, tc), lambda i, j: (i, j)),
        out_shape=jax.ShapeDtypeStruct((t, n_ch), BF16),
        scratch_shapes=[pltpu.VMEM((halo + ts, tc), F32)],
        compiler_params=_params(("arbitrary", "arbitrary")),
        name="dwconv_silu",
    )(x, x, w.astype(F32))


def _dwconv_ln_kernel(cur_ref, halo_ref, w_ref, b_ref, lg_ref, lb_ref, o_ref, buf_ref, acc_ref,
                      *, width, halo, tpb, rows, cw):
    ts, nch = cur_ref.shape
    _fill_conv_buffer(buf_ref, cur_ref, halo_ref, halo, tpb)
    base = halo - (width - 1)
    n_cc = nch // cw

    def conv_block(it, carry):
        r0 = pl.multiple_of((it // n_cc) * rows, rows)
        c0 = pl.multiple_of((it % n_cc) * cw, cw)
        blk = buf_ref[pl.ds(r0, rows + halo), pl.ds(c0, cw)]
        acc = None
        for kk in range(width):
            term = blk[base + kk:base + kk + rows] * w_ref[kk:kk + 1, pl.ds(c0, cw)]
            acc = term if acc is None else acc + term
        acc_ref[pl.ds(r0, rows), pl.ds(c0, cw)] = acc + b_ref[:, pl.ds(c0, cw)]
        return carry

    lax.fori_loop(0, (ts // rows) * n_cc, conv_block, 0)

    ln_rows = 16

    def norm_block(it, carry):
        r0 = pl.multiple_of(it * ln_rows, ln_rows)
        a = acc_ref[pl.ds(r0, ln_rows), :]
        mu = jnp.mean(a, axis=-1, keepdims=True)
        cen = a - mu
        var = jnp.mean(cen * cen, axis=-1, keepdims=True)
        y = cen * lax.rsqrt(var + _EPS) * lg_ref[...] + lb_ref[...]
        o_ref[pl.ds(r0, ln_rows), :] = (y * _sigmoid(y)).astype(o_ref.dtype)
        return carry

    lax.fori_loop(0, ts // ln_rows, norm_block, 0)


def _dwconv_ln_call(x, w, bias, ln_g, ln_b, seq, ts_pref=128, rows=32, cw=256):
    t, n_ch = x.shape
    width = w.shape[0]
    halo = _conv_halo_rows(width)
    ts = _pick(seq, ts_pref)
    cw = _pick(n_ch, cw)
    rows = min(rows, ts)
    assert ts % halo == 0 and ts % rows == 0 and ts % 16 == 0
    tpb = seq // ts
    hb = ts // halo

    def row(v):
        return v.astype(F32).reshape(1, n_ch)

    return pl.pallas_call(
        functools.partial(_dwconv_ln_kernel, width=width, halo=halo, tpb=tpb, rows=rows, cw=cw),
        grid=(t // ts,),
        in_specs=[pl.BlockSpec((ts, n_ch), lambda i: (i, 0)),
                  pl.BlockSpec((halo, n_ch), lambda i: (jnp.maximum(i * hb - 1, 0), 0)),
                  pl.BlockSpec((width, n_ch), lambda i: (0, 0)),
                  pl.BlockSpec((1, n_ch), lambda i: (0, 0)),
                  pl.BlockSpec((1, n_ch), lambda i: (0, 0)),
                  pl.BlockSpec((1, n_ch), lambda i: (0, 0))],
        out_specs=pl.BlockSpec((ts, n_ch), lambda i: (i, 0)),
        out_shape=jax.ShapeDtypeStruct((t, n_ch), BF16),
        scratch_shapes=[pltpu.VMEM((halo + ts, n_ch), F32), pltpu.VMEM((ts, n_ch), F32)],
        compiler_params=_params(("arbitrary",)),
        name="dwconv_ln",
    )(x, x, w.astype(F32), row(bias), row(ln_g), row(ln_b))


def _delta_kernel(q_ref, k_ref, v_ref, z_ref, bg_ref, grow_ref, ng_ref, o_ref, s_ref,
                  *, hb, rep, dh, hv_total, scale):
    @pl.when(pl.program_id(2) == 0)
    def _():
        s_ref[...] = jnp.zeros(s_ref.shape, F32)

    ts = q_ref.shape[0]
    nc = ts // _CHUNK
    bg = bg_ref[...]
    lane = lax.broadcasted_iota(jnp.int32, bg.shape, 1)
    ri = lax.broadcasted_iota(jnp.int32, (ts, ts), 0)
    ci = lax.broadcasted_iota(jnp.int32, (ts, ts), 1)
    same = (ri // _CHUNK) == (ci // _CHUNK)
    causal = jnp.logical_and(same, ri >= ci)
    strict = jnp.logical_and(same, ri > ci)
    eye = (ri == ci).astype(F32)
    ng = ng_ref[...]
    n_sq = _CHUNK.bit_length() - 2

    for hl in range(hb):
        q = q_ref[:, hl * dh:(hl + 1) * dh].astype(F32)
        k = k_ref[:, hl * dh:(hl + 1) * dh].astype(F32)
        qn = q * lax.rsqrt(jnp.sum(q * q, axis=-1, keepdims=True) + _EPS) * scale
        kn = k * lax.rsqrt(jnp.sum(k * k, axis=-1, keepdims=True) + _EPS)
        kb = kn.astype(BF16)
        kk = _dot_nt(kb, kb)
        qk = _dot_nt(qn.astype(BF16), kb)
        for r in range(rep):
            hi = hl * rep + r
            hv = (pl.program_id(1) * hb + hl) * rep + r
            bcol = jnp.sum(jnp.where(lane == hv, bg, 0.0), axis=-1, keepdims=True)
            gcol = jnp.sum(jnp.where(lane == hv_total + hv, bg, 0.0), axis=-1, keepdims=True)
            grow = grow_ref[hi]
            v = v_ref[:, hi * dh:(hi + 1) * dh].astype(F32)
            z = z_ref[:, hi * dh:(hi + 1) * dh].astype(F32)
            decay = jnp.where(causal, jnp.exp(jnp.where(causal, gcol - grow, 0.0)), 0.0)
            x = -jnp.where(strict, kk * bcol * decay, 0.0)
            tinv = eye + x
            xp = x
            for _ in range(n_sq):
                xpb = xp.astype(BF16)
                xp = _dot(xpb, xpb)
                tinv = tinv + _dot(tinv.astype(BF16), xp.astype(BF16))
            eg = jnp.exp(gcol)
            rhs = jnp.concatenate([v * bcol, kn * (bcol * eg)], axis=1).astype(BF16)
            sol = _dot(tinv.astype(BF16), rhs).astype(BF16)
            qx = _dot((qk * decay).astype(BF16), sol)
            q_eff = (qn * eg - qx[:, dh:]).astype(BF16)
            o_loc = qx[:, :dh]
            glast = jnp.concatenate(
                [jnp.broadcast_to(gcol[(c + 1) * _CHUNK - 1:(c + 1) * _CHUNK, :], (_CHUNK, 1))
                 for c in range(nc)], axis=0)
            k_dec = (kn * jnp.exp(glast - gcol)).astype(BF16)
            state = s_ref[hi]
            for c in range(nc):
                sl = slice(c * _CHUNK, (c + 1) * _CHUNK)
                kx = _dot_tn(k_dec[sl], sol[sl])
                sb = state.astype(BF16)
                o = _dot(q_eff[sl], sb) + o_loc[sl]
                gl = jnp.exp(gcol[(c + 1) * _CHUNK - 1:(c + 1) * _CHUNK, :])
                state = gl * state - _dot(kx[:, dh:].astype(BF16), sb) + kx[:, :dh]
                on = o * lax.rsqrt(jnp.mean(o * o, axis=-1, keepdims=True) + _EPS) * ng
                zc = z[sl]
                o_ref[sl, hi * dh:(hi + 1) * dh] = (on * (zc * _sigmoid(zc))).astype(o_ref.dtype)
            s_ref[hi] = state


def _delta_call(qkv, proj, bg, bgt3, norm_g, bsz, seq, hk, hv, dh, hb=1):
    t = qkv.shape[0]
    rep = hv // hk
    assert hv % hk == 0 and hk % hb == 0 and (2 * hk) % (hb * rep) == 0 and dh % _LANES == 0
    conv_dim = qkv.shape[1]
    ts = _pick(seq, 256)
    assert ts % _CHUNK == 0
    ns = seq // ts
    nh = hb * rep
    v_blk0 = (2 * hk) // nh
    z_blk0 = conv_dim // (nh * dh)
    return pl.pallas_call(
        functools.partial(_delta_kernel, hb=hb, rep=rep, dh=dh, hv_total=hv,
                          scale=float(dh) ** -0.5),
        grid=(bsz, hk // hb, ns),
        in_specs=[pl.BlockSpec((ts, hb * dh), lambda b, h, s: (b * ns + s, h)),
                  pl.BlockSpec((ts, hb * dh), lambda b, h, s: (b * ns + s, hk // hb + h)),
                  pl.BlockSpec((ts, nh * dh), lambda b, h, s: (b * ns + s, v_blk0 + h)),
                  pl.BlockSpec((ts, nh * dh), lambda b, h, s: (b * ns + s, z_blk0 + h)),
                  pl.BlockSpec((ts, _LANES), lambda b, h, s: (b * ns + s, 0)),
                  pl.BlockSpec((nh, 1, ts), lambda b, h, s: (hv // nh + h, 0, b * ns + s)),
                  pl.BlockSpec((1, dh), lambda b, h, s: (0, 0))],
        out_specs=pl.BlockSpec((ts, nh * dh), lambda b, h, s: (b * ns + s, h)),
        out_shape=jax.ShapeDtypeStruct((t, hv * dh), BF16),
        scratch_shapes=[pltpu.VMEM((nh, dh, dh), F32)],
        compiler_params=_params(("arbitrary", "arbitrary", "arbitrary")),
        name="gdn_delta",
    )(qkv, qkv, qkv, proj, bg, bgt3, norm_g.astype(F32).reshape(1, dh))


def _gather_rows_kernel(src_ref, nu_ref, x_hbm, o_ref, buf_ref, sem_ref, *, tm):
    t = pl.program_id(0)
    n_used = nu_ref[0]
    slot = t % 2

    def row_copy(tile, slot_, r):
        return pltpu.make_async_copy(x_hbm.at[pl.ds(src_ref[tile * tm + r], 1), :],
                                     buf_ref.at[slot_, pl.ds(r, 1), :], sem_ref.at[slot_])

    def issue(tile, slot_):
        def body(r, carry):
            row_copy(tile, slot_, r).start()
            return carry
        lax.fori_loop(0, tm, body, 0, unroll=8)

    @pl.when(jnp.logical_and(t == 0, n_used > 0))
    def _():
        issue(0, 0)

    @pl.when(t + 1 < n_used)
    def _():
        issue(t + 1, 1 - slot)

    @pl.when(t < n_used)
    def _():
        def body(r, carry):
            row_copy(t, slot, r).wait()
            return carry
        lax.fori_loop(0, tm, body, 0, unroll=8)
        o_ref[...] = buf_ref[slot].astype(o_ref.dtype)

    @pl.when(t >= n_used)
    def _():
        o_ref[...] = jnp.zeros(o_ref.shape, o_ref.dtype)


def _gather_rows_call(src, nu, x, n_rows, tm):
    d = x.shape[1]
    return pl.pallas_call(
        functools.partial(_gather_rows_kernel, tm=tm),
        grid_spec=pltpu.PrefetchScalarGridSpec(
            num_scalar_prefetch=2,
            grid=(n_rows // tm,),
            in_specs=[pl.BlockSpec(memory_space=pl.ANY)],
            out_specs=pl.BlockSpec((tm, d), lambda t, src, nu: (t, 0)),
            scratch_shapes=[pltpu.VMEM((2, tm, d), x.dtype), pltpu.SemaphoreType.DMA((2,))]),
        out_shape=jax.ShapeDtypeStruct((n_rows, d), BF16),
        compiler_params=_params(("arbitrary",)),
        name="moe_gather",
    )(src, nu, x)


def _expert_changed(te_ref, t):
    return jnp.logical_or(t == 0, te_ref[t] != te_ref[jnp.maximum(t - 1, 0)])


def _moe_up_kernel(te_ref, nu_ref, x_ref, wg_ref, wu_ref, o_ref, wgb_ref, wub_ref):
    t = pl.program_id(1)

    @pl.when(_expert_changed(te_ref, t))
    def _():
        wgb_ref[...] = wg_ref[...].astype(BF16)
        wub_ref[...] = wu_ref[...].astype(BF16)

    @pl.when(t < nu_ref[0])
    def _():
        x = x_ref[...]
        g = _dot(x, wgb_ref[...])
        u = _dot(x, wub_ref[...])
        o_ref[...] = (g * _sigmoid(g) * u).astype(o_ref.dtype)

    @pl.when(t >= nu_ref[0])
    def _():
        o_ref[...] = jnp.zeros(o_ref.shape, o_ref.dtype)


def _moe_up_call(te, nu, xs, w_gate, w_up, layer, tm, tf_pref=512):
    n_rows, d = xs.shape
    f = w_gate.shape[3]
    tf = _pick(f, tf_pref)
    return pl.pallas_call(
        _moe_up_kernel,
        grid_spec=pltpu.PrefetchScalarGridSpec(
            num_scalar_prefetch=2,
            grid=(f // tf, n_rows // tm),
            in_specs=[pl.BlockSpec((tm, d), lambda j, t, te, nu: (t, 0)),
                      pl.BlockSpec((None, None, d, tf), lambda j, t, te, nu: (layer, te[t], 0, j)),
                      pl.BlockSpec((None, None, d, tf), lambda j, t, te, nu: (layer, te[t], 0, j))],
            out_specs=pl.BlockSpec((tm, tf), lambda j, t, te, nu: (t, j)),
            scratch_shapes=[pltpu.VMEM((d, tf), BF16), pltpu.VMEM((d, tf), BF16)]),
        out_shape=jax.ShapeDtypeStruct((n_rows, f), BF16),
        compiler_params=_params(("arbitrary", "arbitrary")),
        name="moe_up",
    )(te, nu, xs, w_gate, w_up)


def _moe_down_kernel(te_ref, nu_ref, h_ref, wd_ref, o_ref, wdb_ref):
    t = pl.program_id(1)

    @pl.when(_expert_changed(te_ref, t))
    def _():
        wdb_ref[...] = wd_ref[...].astype(BF16)

    @pl.when(t < nu_ref[0])
    def _():
        o_ref[...] = _dot(h_ref[...], wdb_ref[...])

    @pl.when(t >= nu_ref[0])
    def _():
        o_ref[...] = jnp.zeros(o_ref.shape, o_ref.dtype)


def _moe_down_call(te, nu, hmid, w_down, layer, tm, tn_pref=2048):
    n_rows, f = hmid.shape
    d = w_down.shape[3]
    tn = _pick(d, tn_pref)
    return pl.pallas_call(
        _moe_down_kernel,
        grid_spec=pltpu.PrefetchScalarGridSpec(
            num_scalar_prefetch=2,
            grid=(d // tn, n_rows // tm),
            in_specs=[pl.BlockSpec((tm, f), lambda j, t, te, nu: (t, 0)),
                      pl.BlockSpec((None, None, f, tn), lambda j, t, te, nu: (layer, te[t], 0, j))],
            out_specs=pl.BlockSpec((tm, tn), lambda j, t, te, nu: (t, j)),
            scratch_shapes=[pltpu.VMEM((f, tn), BF16)]),
        out_shape=jax.ShapeDtypeStruct((n_rows, d), F32),
        compiler_params=_params(("arbitrary", "arbitrary")),
        name="moe_down",
    )(te, nu, hmid, w_down)


def _moe_combine_kernel(pos_ref, x_ref, wt_ref, gt_ref, ys_hbm, o_ref, buf_ref, sem, *, tm):
    base = pl.program_id(0) * tm

    def row_copy(r, kk):
        return pltpu.make_async_copy(ys_hbm.at[pl.ds(pos_ref[(base + r) * 2 + kk], 1), :],
                                     buf_ref.at[kk, pl.ds(r, 1), :], sem)

    def issue(r, carry):
        row_copy(r, 0).start()
        row_copy(r, 1).start()
        return carry

    def drain(r, carry):
        row_copy(r, 0).wait()
        row_copy(r, 1).wait()
        return carry

    lax.fori_loop(0, tm, issue, 0)
    lax.fori_loop(0, tm, drain, 0)
    wt = wt_ref[...]
    y = wt[:, 0:1] * buf_ref[0] + wt[:, 1:2] * buf_ref[1]
    o_ref[...] = x_ref[...] + gt_ref[...] * y


def _moe_combine_call(pos, x, wt, modl, gate_idx, ys, seq):
    t, d = x.shape
    tm = _pick(seq, 128)
    tpb = seq // tm
    return pl.pallas_call(
        functools.partial(_moe_combine_kernel, tm=tm),
        grid_spec=pltpu.PrefetchScalarGridSpec(
            num_scalar_prefetch=1,
            grid=(t // tm,),
            in_specs=[pl.BlockSpec((tm, d), lambda i, pos: (i, 0)),
                      pl.BlockSpec((tm, 2), lambda i, pos: (i, 0)),
                      pl.BlockSpec((None, None, 1, d), lambda i, pos: (i // tpb, gate_idx, 0, 0)),
                      pl.BlockSpec(memory_space=pl.ANY)],
            out_specs=pl.BlockSpec((tm, d), lambda i, pos: (i, 0)),
            scratch_shapes=[pltpu.VMEM((2, tm, d), F32), pltpu.SemaphoreType.DMA(())]),
        out_shape=jax.ShapeDtypeStruct((t, d), F32),
        compiler_params=_params(("arbitrary",)),
        name="moe_combine",
    )(pos, x, wt, modl, ys)


def _route_tables(idx, n_exp, tm):
    n_sel = idx.shape[0] * idx.shape[1]
    e_flat = idx.reshape(-1)
    onehot = (e_flat[:, None] == jnp.arange(n_exp, dtype=jnp.int32)[None, :]).astype(jnp.int32)
    csum = jnp.cumsum(onehot, axis=0)
    counts = csum[-1]
    rank = jnp.sum(onehot * (csum - 1), axis=1)
    padded = ((counts + tm - 1) // tm) * tm
    ends = jnp.cumsum(padded)
    starts = ends - padded
    pos = jnp.sum(onehot * starts[None, :], axis=1) + rank
    n_rows = -(-n_sel // tm) * tm + n_exp * tm
    src = jnp.zeros((n_rows,), jnp.int32).at[pos].set(
        jnp.arange(n_sel, dtype=jnp.int32) // idx.shape[1])
    n_tiles = n_rows // tm
    n_used = (ends[-1] // tm).astype(jnp.int32)
    tile = jnp.minimum(jnp.arange(n_tiles, dtype=jnp.int32), n_used - 1)
    te = jnp.sum((tile[:, None] * tm >= ends[None, :]).astype(jnp.int32), axis=1)
    te = jnp.minimum(te, n_exp - 1).astype(jnp.int32)
    return pos.astype(jnp.int32), src, te, n_used.reshape(1), n_rows


def _moe_block(x, norm_g, modl, seq, router_w, router_bias, w_gate, w_up, w_down, layer):
    n_exp = router_w.shape[1]
    h, idx_t, wt_t = _moe_norm_router_call(x, norm_g, modl, seq, router_w, router_bias)
    idx = idx_t[:2].T
    wt = wt_t[:2].T
    tm = 256
    pos, src, te, nu, n_rows = _route_tables(idx, n_exp, tm)
    xs = _gather_rows_call(src, nu, h, n_rows, tm)
    hmid = _moe_up_call(te, nu, xs, w_gate, w_up, layer, tm)
    ys = _moe_down_call(te, nu, hmid, w_down, layer, tm)
    return _moe_combine_call(pos, x, wt, modl, 5, ys, seq)


def _gdn_mixer(x, norm_g, modl, bsz, seq, w_in, conv_w, a_log, dt_bias, gnorm_g, w_out):
    d = x.shape[1]
    hv = a_log.shape[0]
    dh = gnorm_g.shape[0]
    conv_dim = conv_w.shape[1]
    value_dim = hv * dh
    key_dim = (conv_dim - value_dim) // 2
    hk = key_dim // dh
    assert w_in.shape[1] == conv_dim + value_dim + 2 * hv
    h = _norm_mod_call(x, norm_g, modl, 1, 0, seq, BF16)
    proj = _mm_plain(h, w_in, conv_dim + value_dim, BF16)
    bg, bgt = _gdn_ba_call(h, w_in[:, conv_dim + value_dim:], a_log, dt_bias, hv)
    qkv = _dwconv_silu_call(proj, conv_dim, conv_w, seq)
    o = _delta_call(qkv, proj, bg, bgt.reshape(_LANES, 1, -1), gnorm_g, bsz, seq, hk, hv, dh)
    return _mm_resid(o, w_out, jnp.zeros((d,), F32), x, modl, 2, seq)


def _conformer_mixer(x, norm_g, modl, seq, w_in, b_in, dw_w, dw_b, ln_g, ln_b, w_out, b_out):
    inner = w_in.shape[1] // 2
    h = _norm_mod_call(x, norm_g, modl, 1, 0, seq, BF16)
    u = _mm_glu(h, w_in, b_in, BF16)
    u = _dwconv_ln_call(u, dw_w, dw_b, ln_g, ln_b, seq)
    return _mm_resid(u, w_out, b_out, x, modl, 2, seq)


def kernel(x, c, ada_w, ada_b, norm_g, gdn_w_in, gdn_conv_w, gdn_a_log, gdn_dt_bias, gdn_norm_g, gdn_w_out, conf_w_in, conf_b_in, conf_dw_w, conf_dw_b, conf_ln_g, conf_ln_b, conf_w_out, conf_b_out, router_w, router_bias, moe_w_gate, moe_w_up, moe_w_down, final_norm_g):
    bsz, seq, d = x.shape
    depth = ada_w.shape[0]
    mod = _adaln(c, ada_w, ada_b)
    xf = x.reshape(bsz * seq, d)
    for i in range(depth):
        j = i // 2
        if i % 2 == 0:
            xf = _gdn_mixer(xf, norm_g[i, 0], mod[i], bsz, seq, gdn_w_in[j], gdn_conv_w[j],
                            gdn_a_log[j], gdn_dt_bias[j], gdn_norm_g[j], gdn_w_out[j])
        else:
            xf = _conformer_mixer(xf, norm_g[i, 0], mod[i], seq, conf_w_in[j], conf_b_in[j],
                                  conf_dw_w[j], conf_dw_b[j], conf_ln_g[j], conf_ln_b[j],
                                  conf_w_out[j], conf_b_out[j])
        xf = _moe_block(xf, norm_g[i, 1], mod[i], seq, router_w, router_bias,
                        moe_w_gate, moe_w_up, moe_w_down, i)
    return _final_norm_call(xf, final_norm_g).reshape(bsz, seq, d)
```

```python
import functools

import jax
import jax.numpy as jnp
from jax import lax
from jax.experimental import pallas as pl
from jax.experimental.pallas import tpu as pltpu

F32 = jnp.float32
BF16 = jnp.bfloat16

_EPS = 1e-6
_CHUNK = 64
_N_GROUPS = 4
_N_MOD = 6
_LANES = 128
_VMEM_LIMIT = 56 * 1024 * 1024


def _pick(dim, pref):
    t = min(pref, dim)
    while dim % t:
        t //= 2
    return max(t, 1)


def _params(sem, vmem=_VMEM_LIMIT):
    return pltpu.CompilerParams(dimension_semantics=sem, vmem_limit_bytes=vmem)


def _sigmoid(x):
    return 1.0 / (1.0 + jnp.exp(-x))


def _dot(a, b):
    return jnp.dot(a, b, preferred_element_type=F32)


def _dot_nt(a, b):
    return lax.dot_general(a, b, (((1,), (1,)), ((), ())), preferred_element_type=F32)


def _dot_tn(a, b):
    return lax.dot_general(a, b, (((0,), (0,)), ((), ())), preferred_element_type=F32)


def _adaln_kernel(c_ref, w_ref, b_ref, o_ref):
    c = c_ref[...]
    ca = (c * _sigmoid(c)).astype(BF16)
    o_ref[...] = _dot(ca, w_ref[...].astype(BF16)) + b_ref[...]


def _adaln(c, ada_w, ada_b):
    depth, d, n = ada_w.shape
    bsz = c.shape[0]
    bp = -(-bsz // 8) * 8
    c8 = jnp.pad(c, ((0, bp - bsz), (0, 0)))
    tn = _pick(n, 512)
    out = pl.pallas_call(
        _adaln_kernel,
        grid=(depth, n // tn),
        in_specs=[pl.BlockSpec((bp, d), lambda l, j: (0, 0)),
                  pl.BlockSpec((None, d, tn), lambda l, j: (l, 0, j)),
                  pl.BlockSpec((None, 1, tn), lambda l, j: (l, 0, j))],
        out_specs=pl.BlockSpec((None, bp, tn), lambda l, j: (l, 0, j)),
        out_shape=jax.ShapeDtypeStruct((depth, bp, n), F32),
        compiler_params=_params(("arbitrary", "arbitrary")),
        name="adaln",
    )(c8, ada_w, ada_b.reshape(depth, 1, n))
    return out[:, :bsz].reshape(depth, bsz, _N_MOD, 1, d)


def _norm_mod(x, g, sc, sh):
    ms = jnp.mean(x * x, axis=-1, keepdims=True)
    return x * lax.rsqrt(ms + _EPS) * g * (1.0 + sc) + sh


def _norm_mod_kernel(x_ref, g_ref, sc_ref, sh_ref, o_ref):
    o_ref[...] = _norm_mod(x_ref[...], g_ref[...], sc_ref[...], sh_ref[...]).astype(o_ref.dtype)


def _norm_mod_call(x, g, modl, sc_idx, sh_idx, seq, out_dtype):
    t, d = x.shape
    tm = _pick(seq, 256)
    tpb = seq // tm
    return pl.pallas_call(
        _norm_mod_kernel,
        grid=(t // tm,),
        in_specs=[pl.BlockSpec((tm, d), lambda i: (i, 0)),
                  pl.BlockSpec((1, d), lambda i: (0, 0)),
                  pl.BlockSpec((None, None, 1, d), lambda i: (i // tpb, sc_idx, 0, 0)),
                  pl.BlockSpec((None, None, 1, d), lambda i: (i // tpb, sh_idx, 0, 0))],
        out_specs=pl.BlockSpec((tm, d), lambda i: (i, 0)),
        out_shape=jax.ShapeDtypeStruct((t, d), out_dtype),
        compiler_params=_params(("arbitrary",)),
        name="norm_mod",
    )(x, g.reshape(1, d), modl, modl)


def _top2_sum(v):
    a = jnp.maximum(v[0], v[1]); b = jnp.minimum(v[0], v[1])
    c = jnp.maximum(v[2], v[3]); d = jnp.minimum(v[2], v[3])
    return jnp.maximum(a, c) + jnp.maximum(jnp.minimum(a, c), jnp.maximum(b, d))


def _first_argmax(vals):
    best = vals[0]
    idx = jnp.zeros(best.shape, jnp.int32)
    for e in range(1, len(vals)):
        take = vals[e] > best
        idx = jnp.where(take, e, idx)
        best = jnp.where(take, vals[e], best)
    return idx, best


def _moe_norm_router_kernel(x_ref, g_ref, sc_ref, sh_ref, rw_ref, rb_ref,
                            h_ref, idx_ref, wt_ref, *, n_exp):
    h = _norm_mod(x_ref[...], g_ref[...], sc_ref[...], sh_ref[...])
    h_ref[...] = h
    logits = jnp.dot(h, rw_ref[...], preferred_element_type=F32,
                     precision=lax.Precision.HIGHEST)
    lt = logits.T
    per_group = n_exp // _N_GROUPS
    aff = [_sigmoid(lt[e:e + 1, :]) for e in range(n_exp)]
    biased = [aff[e] + rb_ref[e] for e in range(n_exp)]
    gscore = [_top2_sum(biased[gi * per_group:(gi + 1) * per_group]) for gi in range(_N_GROUPS)]
    best_group, _ = _first_argmax(gscore)
    neg = jnp.full(aff[0].shape, -jnp.inf, F32)
    masked = [jnp.where(best_group == (e // per_group), biased[e], neg) for e in range(n_exp)]
    i1, _ = _first_argmax(masked)
    masked2 = [jnp.where(i1 == e, neg, masked[e]) for e in range(n_exp)]
    i2, _ = _first_argmax(masked2)
    zero = jnp.zeros(aff[0].shape, F32)
    a1 = zero
    a2 = zero
    for e in range(n_exp):
        a1 = a1 + jnp.where(i1 == e, aff[e], zero)
        a2 = a2 + jnp.where(i2 == e, aff[e], zero)
    den = a1 + a2
    idx_ref[...] = jnp.zeros(idx_ref.shape, jnp.int32)
    wt_ref[...] = jnp.zeros(wt_ref.shape, F32)
    idx_ref[0:1, :] = i1
    idx_ref[1:2, :] = i2
    wt_ref[0:1, :] = a1 / den
    wt_ref[1:2, :] = a2 / den


def _moe_norm_router_call(x, g, modl, seq, router_w, router_bias):
    t, d = x.shape
    n_exp = router_w.shape[1]
    assert n_exp % _N_GROUPS == 0 and n_exp // _N_GROUPS == 4 and n_exp <= _LANES
    tm = _pick(seq, 256)
    tpb = seq // tm
    rw = jnp.pad(router_w.astype(F32), ((0, 0), (0, _LANES - n_exp)))
    return pl.pallas_call(
        functools.partial(_moe_norm_router_kernel, n_exp=n_exp),
        grid=(t // tm,),
        in_specs=[pl.BlockSpec((tm, d), lambda i: (i, 0)),
                  pl.BlockSpec((1, d), lambda i: (0, 0)),
                  pl.BlockSpec((None, None, 1, d), lambda i: (i // tpb, 4, 0, 0)),
                  pl.BlockSpec((None, None, 1, d), lambda i: (i // tpb, 3, 0, 0)),
                  pl.BlockSpec((d, _LANES), lambda i: (0, 0)),
                  pl.BlockSpec(memory_space=pltpu.SMEM)],
        out_specs=[pl.BlockSpec((tm, d), lambda i: (i, 0)),
                   pl.BlockSpec((8, tm), lambda i: (0, i)),
                   pl.BlockSpec((8, tm), lambda i: (0, i))],
        out_shape=[jax.ShapeDtypeStruct((t, d), F32),
                   jax.ShapeDtypeStruct((8, t), jnp.int32),
                   jax.ShapeDtypeStruct((8, t), F32)],
        compiler_params=_params(("arbitrary",)),
        name="moe_norm_router",
    )(x, g.reshape(1, d), modl, modl, rw, router_bias.astype(F32))


def _mm_wt_kernel(x_ref, wt_ref, o_ref, wb_ref):
    @pl.when(pl.program_id(1) == 0)
    def _():
        wb_ref[...] = wt_ref[...].T.astype(BF16)
    o_ref[...] = _dot(x_ref[...], wb_ref[...]).astype(o_ref.dtype)


def _mm_wt(x, wt, n_out, out_dtype, tm_pref=1024, tn_pref=512):
    m, k = x.shape
    tm = _pick(m, tm_pref)
    tn = _pick(n_out, tn_pref)
    return pl.pallas_call(
        _mm_wt_kernel,
        grid=(n_out // tn, m // tm),
        in_specs=[pl.BlockSpec((tm, k), lambda j, i: (i, 0)),
                  pl.BlockSpec((tn, k), lambda j, i: (j, 0))],
        out_specs=pl.BlockSpec((tm, tn), lambda j, i: (i, j)),
        out_shape=jax.ShapeDtypeStruct((m, n_out), out_dtype),
        scratch_shapes=[pltpu.VMEM((k, tn), BF16)],
        compiler_params=_params(("arbitrary", "arbitrary")),
        name="mm_wt",
    )(x, wt)


def _mm_resid_kernel(x_ref, w_ref, b_ref, res_ref, gt_ref, o_ref, wb_ref):
    @pl.when(pl.program_id(1) == 0)
    def _():
        wb_ref[...] = w_ref[...].astype(BF16)
    y = _dot(x_ref[...], wb_ref[...]) + b_ref[...]
    o_ref[...] = res_ref[...] + gt_ref[...] * y


def _mm_resid(x, w, bias, res, modl, gate_idx, seq, tm_pref=1024, tn_pref=512):
    m, k = x.shape
    n = w.shape[1]
    tm = _pick(seq, tm_pref)
    tn = _pick(n, tn_pref)
    tpb = seq // tm
    return pl.pallas_call(
        _mm_resid_kernel,
        grid=(n // tn, m // tm),
        in_specs=[pl.BlockSpec((tm, k), lambda j, i: (i, 0)),
                  pl.BlockSpec((k, tn), lambda j, i: (0, j)),
                  pl.BlockSpec((1, tn), lambda j, i: (0, j)),
                  pl.BlockSpec((tm, tn), lambda j, i: (i, j)),
                  pl.BlockSpec((None, None, 1, tn), lambda j, i: (i // tpb, gate_idx, 0, j))],
        out_specs=pl.BlockSpec((tm, tn), lambda j, i: (i, j)),
        out_shape=jax.ShapeDtypeStruct((m, n), F32),
        scratch_shapes=[pltpu.VMEM((k, tn), BF16)],
        compiler_params=_params(("arbitrary", "arbitrary")),
        name="mm_resid",
    )(x, w, bias.reshape(1, n), res, modl)


def _mm_glu_kernel(x_ref, wv_ref, wg_ref, bv_ref, bg_ref, o_ref, wvb_ref, wgb_ref):
    @pl.when(pl.program_id(1) == 0)
    def _():
        wvb_ref[...] = wv_ref[...].astype(BF16)
        wgb_ref[...] = wg_ref[...].astype(BF16)
    x = x_ref[...]
    val = _dot(x, wvb_ref[...]) + bv_ref[...]
    gate = _dot(x, wgb_ref[...]) + bg_ref[...]
    o_ref[...] = (val * _sigmoid(gate)).astype(o_ref.dtype)


def _mm_glu(x, w, b, out_dtype, tm_pref=1024, tn_pref=256):
    m, k = x.shape
    n = w.shape[1] // 2
    tm = _pick(m, tm_pref)
    tn = _pick(n, tn_pref)
    nb = n // tn
    b2 = b.reshape(1, 2 * n)
    return pl.pallas_call(
        _mm_glu_kernel,
        grid=(nb, m // tm),
        in_specs=[pl.BlockSpec((tm, k), lambda j, i: (i, 0)),
                  pl.BlockSpec((k, tn), lambda j, i: (0, j)),
                  pl.BlockSpec((k, tn), lambda j, i: (0, j + nb)),
                  pl.BlockSpec((1, tn), lambda j, i: (0, j)),
                  pl.BlockSpec((1, tn), lambda j, i: (0, j + nb))],
        out_specs=pl.BlockSpec((tm, tn), lambda j, i: (i, j)),
        out_shape=jax.ShapeDtypeStruct((m, n), out_dtype),
        scratch_shapes=[pltpu.VMEM((k, tn), BF16), pltpu.VMEM((k, tn), BF16)],
        compiler_params=_params(("arbitrary", "arbitrary")),
        name="mm_glu",
    )(x, w, w, b2, b2)


def _gdn_ba_kernel(h_ref, w_ref, al_ref, dt_ref, o_ref, ot_ref, *, hv):
    p = _dot_nt(h_ref[...], w_ref[...].astype(BF16))
    tm = p.shape[0]
    lane = lax.broadcasted_iota(jnp.int32, p.shape, 1)
    beta = _sigmoid(p)
    a = p + dt_ref[...]
    softplus = jnp.maximum(a, 0.0) + jnp.log1p(jnp.exp(-jnp.abs(a)))
    g = -jnp.exp(al_ref[...]) * softplus
    ri = lax.broadcasted_iota(jnp.int32, (_CHUNK, _CHUNK), 0)
    ci = lax.broadcasted_iota(jnp.int32, (_CHUNK, _CHUNK), 1)
    tril = (ri >= ci).astype(F32)
    parts = [jnp.dot(tril, g[c * _CHUNK:(c + 1) * _CHUNK], preferred_element_type=F32,
                     precision=lax.Precision.HIGHEST) for c in range(tm // _CHUNK)]
    gcum = jnp.concatenate(parts, axis=0)
    out = jnp.where(lane < hv, beta, jnp.where(lane < 2 * hv, gcum, 0.0))
    o_ref[...] = out
    ot_ref[...] = out.T


def _gdn_ba_call(h, w_ba_t, a_log, dt_bias, hv):
    t, d = h.shape
    assert 2 * hv <= _LANES
    pad = _LANES - 2 * hv
    w128 = jnp.pad(w_ba_t, ((0, pad), (0, 0)))
    al = jnp.pad(a_log.astype(F32), (hv, pad)).reshape(1, _LANES)
    dt = jnp.pad(dt_bias.astype(F32), (hv, pad)).reshape(1, _LANES)
    tm = _pick(t, 512)
    assert tm % _CHUNK == 0
    return pl.pallas_call(
        functools.partial(_gdn_ba_kernel, hv=hv),
        grid=(t // tm,),
        in_specs=[pl.BlockSpec((tm, d), lambda i: (i, 0)),
                  pl.BlockSpec((_LANES, d), lambda i: (0, 0)),
                  pl.BlockSpec((1, _LANES), lambda i: (0, 0)),
                  pl.BlockSpec((1, _LANES), lambda i: (0, 0))],
        out_specs=[pl.BlockSpec((tm, _LANES), lambda i: (i, 0)),
                   pl.BlockSpec((_LANES, tm), lambda i: (0, i))],
        out_shape=[jax.ShapeDtypeStruct((t, _LANES), F32),
                   jax.ShapeDtypeStruct((_LANES, t), F32)],
        compiler_params=_params(("arbitrary",)),
        name="gdn_ba",
    )(h, w128, al, dt)


def _conv_halo_rows(width):
    return 16 * (-(-(width - 1) // 16))


def _fill_conv_buffer(buf_ref, cur_ref, halo_ref, halo, tpb):
    ts = cur_ref.shape[0]
    first = (pl.program_id(0) % tpb) == 0
    buf_ref[0:halo, :] = jnp.where(first, 0.0, halo_ref[...].astype(F32))
    buf_ref[halo:halo + ts, :] = cur_ref[...].astype(F32)


def _dwconv_silu_kernel(cur_ref, halo_ref, w_ref, o_ref, buf_ref, *, width, halo, tpb, rows):
    ts = cur_ref.shape[0]
    _fill_conv_buffer(buf_ref, cur_ref, halo_ref, halo, tpb)
    base = halo - (width - 1)
    for r0 in range(0, ts, rows):
        acc = None
        for kk in range(width):
            term = buf_ref[base + r0 + kk:base + r0 + kk + rows, :] * w_ref[kk:kk + 1, :]
            acc = term if acc is None else acc + term
        o_ref[r0:r0 + rows, :] = (acc * _sigmoid(acc)).astype(o_ref.dtype)


def _dwconv_silu_call(x, n_ch, w, seq, ts_pref=512, tc_pref=512, rows=64):
    t = x.shape[0]
    width = w.shape[0]
    halo = _conv_halo_rows(width)
    ts = _pick(seq, ts_pref)
    tc = _pick(n_ch, tc_pref)
    rows = min(rows, ts)
    assert ts % halo == 0 and ts % rows == 0
    tpb = seq // ts
    hb = ts // halo
    return pl.pallas_call(
        functools.partial(_dwconv_silu_kernel, width=width, halo=halo, tpb=tpb, rows=rows),
        grid=(t // ts, n_ch // tc),
        in_specs=[pl.BlockSpec((ts, tc), lambda i, j: (i, j)),
                  pl.BlockSpec((halo, tc), lambda i, j: (jnp.maximum(i * hb - 1, 0), j)),
                  pl.BlockSpec((width, tc), lambda i, j: (0, j))],
        out_specs=pl.BlockSpec((ts, tc), lambda i, j: (i, j)),
        out_shape=jax.ShapeDtypeStruct((t, n_ch), BF16),
        scratch_shapes=[pltpu.VMEM((halo + ts, tc), F32)],
        compiler_params=_params(("arbitrary", "arbitrary")),
        name="dwconv_silu",
    )(x, x, w.astype(F32))


def _dwconv_ln_kernel(cur_ref, halo_ref, w_ref, b_ref, lg_ref, lb_ref, o_ref, buf_ref, sh_ref,
                      acc_ref, *, width, halo, tpb, rows, cw):
    ts, nch = cur_ref.shape
    _fill_conv_buffer(buf_ref, cur_ref, halo_ref, halo, tpb)
    base = halo - (width - 1)
    n_cc = nch // cw
    n_sh = halo + ts - 8

    def shift_block(cc, carry):
        c0 = pl.multiple_of(cc * cw, cw)
        blk = buf_ref[:, pl.ds(c0, cw)]
        for s in range(1, 8):
            sh_ref[s - 1, :, pl.ds(c0, cw)] = blk[s:s + n_sh]
        return carry

    lax.fori_loop(0, n_cc, shift_block, 0)

    def conv_block(it, carry):
        r0 = pl.multiple_of((it // n_cc) * rows, rows)
        c0 = pl.multiple_of((it % n_cc) * cw, cw)
        acc = None
        for kk in range(width):
            a, s = divmod(base + kk, 8)
            if s == 0:
                win = buf_ref[pl.ds(r0 + 8 * a, rows), pl.ds(c0, cw)]
            else:
                win = sh_ref[s - 1, pl.ds(r0 + 8 * a, rows), pl.ds(c0, cw)]
            term = win * w_ref[kk:kk + 1, pl.ds(c0, cw)]
            acc = term if acc is None else acc + term
        acc_ref[pl.ds(r0, rows), pl.ds(c0, cw)] = acc + b_ref[:, pl.ds(c0, cw)]
        return carry

    lax.fori_loop(0, (ts // rows) * n_cc, conv_block, 0)

    ln_rows = 16

    def norm_block(it, carry):
        r0 = pl.multiple_of(it * ln_rows, ln_rows)
        a = acc_ref[pl.ds(r0, ln_rows), :]
        mu = jnp.mean(a, axis=-1, keepdims=True)
        cen = a - mu
        var = jnp.mean(cen * cen, axis=-1, keepdims=True)
        y = cen * lax.rsqrt(var + _EPS) * lg_ref[...] + lb_ref[...]
        o_ref[pl.ds(r0, ln_rows), :] = (y * _sigmoid(y)).astype(o_ref.dtype)
        return carry

    lax.fori_loop(0, ts // ln_rows, norm_block, 0)


def _dwconv_ln_call(x, w, bias, ln_g, ln_b, seq, ts_pref=128, rows=128, cw=256):
    t, n_ch = x.shape
    width = w.shape[0]
    halo = _conv_halo_rows(width)
    ts = _pick(seq, ts_pref)
    cw = _pick(n_ch, cw)
    rows = min(rows, ts)
    assert ts % halo == 0 and ts % rows == 0 and ts % 16 == 0
    tpb = seq // ts
    hb = ts // halo

    def row(v):
        return v.astype(F32).reshape(1, n_ch)

    return pl.pallas_call(
        functools.partial(_dwconv_ln_kernel, width=width, halo=halo, tpb=tpb, rows=rows, cw=cw),
        grid=(t // ts,),
        in_specs=[pl.BlockSpec((ts, n_ch), lambda i: (i, 0)),
                  pl.BlockSpec((halo, n_ch), lambda i: (jnp.maximum(i * hb - 1, 0), 0)),
                  pl.BlockSpec((width, n_ch), lambda i: (0, 0)),
                  pl.BlockSpec((1, n_ch), lambda i: (0, 0)),
                  pl.BlockSpec((1, n_ch), lambda i: (0, 0)),
                  pl.BlockSpec((1, n_ch), lambda i: (0, 0))],
        out_specs=pl.BlockSpec((ts, n_ch), lambda i: (i, 0)),
        out_shape=jax.ShapeDtypeStruct((t, n_ch), BF16),
        scratch_shapes=[pltpu.VMEM((halo + ts, n_ch), F32),
                        pltpu.VMEM((7, halo + ts - 8, n_ch), F32),
                        pltpu.VMEM((ts, n_ch), F32)],
        compiler_params=_params(("arbitrary",)),
        name="dwconv_ln",
    )(x, x, w.astype(F32), row(bias), row(ln_g), row(ln_b))


def _delta_kernel(q_ref, k_ref, v_ref, z_ref, bg_ref, grow_ref, ng_ref, o_ref, s_ref,
                  *, hb, rep, dh, hv_total, scale):
    @pl.when(pl.program_id(2) == 0)
    def _():
        s_ref[...] = jnp.zeros(s_ref.shape, F32)

    ts = q_ref.shape[0]
    nc = ts // _CHUNK
    bg = bg_ref[...]
    lane = lax.broadcasted_iota(jnp.int32, bg.shape, 1)
    ri = lax.broadcasted_iota(jnp.int32, (ts, ts), 0)
    ci = lax.broadcasted_iota(jnp.int32, (ts, ts), 1)
    same = (ri // _CHUNK) == (ci // _CHUNK)
    causal = jnp.logical_and(same, ri >= ci)
    strict = jnp.logical_and(same, ri > ci)
    ng = ng_ref[...]
    n_fac = _CHUNK.bit_length() - 1
    heads = [(hl, r) for hl in range(hb) for r in range(rep)]
    nh = len(heads)

    qn_l, kn_l, qk_l, kk_l = [], [], [], []
    for hl in range(hb):
        q = q_ref[:, hl * dh:(hl + 1) * dh].astype(F32)
        k = k_ref[:, hl * dh:(hl + 1) * dh].astype(F32)
        qn = q * lax.rsqrt(jnp.sum(q * q, axis=-1, keepdims=True) + _EPS) * scale
        kn = k * lax.rsqrt(jnp.sum(k * k, axis=-1, keepdims=True) + _EPS)
        kb = kn.astype(BF16)
        qkk = _dot_nt(jnp.concatenate([qn.astype(BF16), kb], axis=0), kb)
        qn_l.append(qn)
        kn_l.append(kn)
        qk_l.append(qkk[:ts])
        kk_l.append(qkk[ts:])

    gcol_l, decay_l, eg_l, q_pow, r_acc = [], [], [], [], []
    for hi, (hl, r) in enumerate(heads):
        hv = (pl.program_id(1) * hb + hl) * rep + r
        bcol = jnp.sum(jnp.where(lane == hv, bg, 0.0), axis=-1, keepdims=True)
        gcol = jnp.sum(jnp.where(lane == hv_total + hv, bg, 0.0), axis=-1, keepdims=True)
        grow = grow_ref[hi]
        v = v_ref[:, hi * dh:(hi + 1) * dh].astype(F32)
        decay = jnp.where(causal, jnp.exp(jnp.where(causal, gcol - grow, 0.0)), 0.0)
        eg = jnp.exp(gcol)
        gcol_l.append(gcol)
        decay_l.append(decay)
        eg_l.append(eg)
        q_pow.append(-jnp.where(strict, kk_l[hl] * bcol * decay, 0.0))
        r_acc.append(jnp.concatenate([v * bcol, kn_l[hl] * (bcol * eg)], axis=1))

    eye = (ri == ci).astype(F32)
    t_inv = [eye + q_pow[hi] for hi in range(nh)]
    for hi in range(nh):
        qb = q_pow[hi].astype(BF16)
        q_pow[hi] = _dot(qb, qb)
    for i in range(1, n_fac):
        for hi in range(nh):
            qb = q_pow[hi].astype(BF16)
            pb = t_inv[hi].astype(BF16)
            if i + 1 < n_fac:
                y = _dot(jnp.concatenate([pb, qb], axis=0), qb)
                t_inv[hi] = t_inv[hi] + y[:ts]
                q_pow[hi] = y[ts:]
            else:
                t_inv[hi] = t_inv[hi] + _dot(pb, qb)
    for hi in range(nh):
        r_acc[hi] = _dot(t_inv[hi].astype(BF16), r_acc[hi].astype(BF16))

    sol_l, qeff_l, oloc_l, kdec_l = [], [], [], []
    for hi, (hl, r) in enumerate(heads):
        sol = r_acc[hi].astype(BF16)
        qx = _dot((qk_l[hl] * decay_l[hi]).astype(BF16), sol)
        gcol = gcol_l[hi]
        glast = jnp.concatenate(
            [jnp.broadcast_to(gcol[(c + 1) * _CHUNK - 1:(c + 1) * _CHUNK, :], (_CHUNK, 1))
             for c in range(nc)], axis=0)
        sol_l.append(sol)
        qeff_l.append((qn_l[hl] * eg_l[hi] - qx[:, dh:]).astype(BF16))
        oloc_l.append(qx[:, :dh])
        kdec_l.append((kn_l[hl] * jnp.exp(glast - gcol)).astype(BF16))

    states = [s_ref[hi] for hi in range(nh)]
    for c in range(nc):
        sl = slice(c * _CHUNK, (c + 1) * _CHUNK)
        for hi in range(nh):
            kx = _dot_tn(kdec_l[hi][sl], sol_l[hi][sl])
            sb = states[hi].astype(BF16)
            y = _dot(jnp.concatenate([qeff_l[hi][sl], kx[:, dh:].astype(BF16)], axis=0), sb)
            o = y[:_CHUNK] + oloc_l[hi][sl]
            gl = jnp.exp(gcol_l[hi][(c + 1) * _CHUNK - 1:(c + 1) * _CHUNK, :])
            states[hi] = gl * states[hi] - y[_CHUNK:] + kx[:, :dh]
            on = o * lax.rsqrt(jnp.mean(o * o, axis=-1, keepdims=True) + _EPS) * ng
            zc = z_ref[sl, hi * dh:(hi + 1) * dh].astype(F32)
            o_ref[sl, hi * dh:(hi + 1) * dh] = (on * (zc * _sigmoid(zc))).astype(o_ref.dtype)
    for hi in range(nh):
        s_ref[hi] = states[hi]


def _delta_call(qkv, proj, bg, bgt3, norm_g, bsz, seq, hk, hv, dh, hb=2, ts_pref=256):
    t = qkv.shape[0]
    rep = hv // hk
    hb = min(hb, hk)
    assert hv % hk == 0 and hk % hb == 0 and (2 * hk) % (hb * rep) == 0 and dh % _LANES == 0
    conv_dim = qkv.shape[1]
    ts = _pick(seq, ts_pref)
    assert ts % _CHUNK == 0
    ns = seq // ts
    nh = hb * rep
    v_blk0 = (2 * hk) // nh
    z_blk0 = conv_dim // (nh * dh)
    return pl.pallas_call(
        functools.partial(_delta_kernel, hb=hb, rep=rep, dh=dh, hv_total=hv,
                          scale=float(dh) ** -0.5),
        grid=(bsz, hk // hb, ns),
        in_specs=[pl.BlockSpec((ts, hb * dh), lambda b, h, s: (b * ns + s, h)),
                  pl.BlockSpec((ts, hb * dh), lambda b, h, s: (b * ns + s, hk // hb + h)),
                  pl.BlockSpec((ts, nh * dh), lambda b, h, s: (b * ns + s, v_blk0 + h)),
                  pl.BlockSpec((ts, nh * dh), lambda b, h, s: (b * ns + s, z_blk0 + h)),
                  pl.BlockSpec((ts, _LANES), lambda b, h, s: (b * ns + s, 0)),
                  pl.BlockSpec((nh, 1, ts), lambda b, h, s: (hv // nh + h, 0, b * ns + s)),
                  pl.BlockSpec((1, dh), lambda b, h, s: (0, 0))],
        out_specs=pl.BlockSpec((ts, nh * dh), lambda b, h, s: (b * ns + s, h)),
        out_shape=jax.ShapeDtypeStruct((t, hv * dh), BF16),
        scratch_shapes=[pltpu.VMEM((nh, dh, dh), F32)],
        compiler_params=_params(("arbitrary", "arbitrary", "arbitrary")),
        name="gdn_delta",
    )(qkv, qkv, qkv, proj, bg, bgt3, norm_g.astype(F32).reshape(1, dh))


def _gather_rows_kernel(src_ref, nu_ref, x_hbm, o_ref, buf_ref, sem_ref, *, tm):
    t = pl.program_id(0)
    n_used = nu_ref[0]
    slot = t % 2

    def row_copy(tile, slot_, r):
        return pltpu.make_async_copy(x_hbm.at[pl.ds(src_ref[tile * tm + r], 1), :],
                                     buf_ref.at[slot_, pl.ds(r, 1), :], sem_ref.at[slot_])

    def issue(tile, slot_):
        def body(r, carry):
            row_copy(tile, slot_, r).start()
            return carry
        lax.fori_loop(0, tm, body, 0, unroll=8)

    @pl.when(jnp.logical_and(t == 0, n_used > 0))
    def _():
        issue(0, 0)

    @pl.when(t + 1 < n_used)
    def _():
        issue(t + 1, 1 - slot)

    @pl.when(t < n_used)
    def _():
        def body(r, carry):
            row_copy(t, slot, r).wait()
            return carry
        lax.fori_loop(0, tm, body, 0, unroll=8)
        o_ref[...] = buf_ref[slot].astype(o_ref.dtype)

    @pl.when(t >= n_used)
    def _():
        o_ref[...] = jnp.zeros(o_ref.shape, o_ref.dtype)


def _gather_rows_call(src, nu, x, n_rows, tm):
    d = x.shape[1]
    return pl.pallas_call(
        functools.partial(_gather_rows_kernel, tm=tm),
        grid_spec=pltpu.PrefetchScalarGridSpec(
            num_scalar_prefetch=2,
            grid=(n_rows // tm,),
            in_specs=[pl.BlockSpec(memory_space=pl.ANY)],
            out_specs=pl.BlockSpec((tm, d), lambda t, src, nu: (t, 0)),
            scratch_shapes=[pltpu.VMEM((2, tm, d), x.dtype), pltpu.SemaphoreType.DMA((2,))]),
        out_shape=jax.ShapeDtypeStruct((n_rows, d), BF16),
        compiler_params=_params(("arbitrary",)),
        name="moe_gather",
    )(src, nu, x)


def _expert_changed(te_ref, t):
    return jnp.logical_or(t == 0, te_ref[t] != te_ref[jnp.maximum(t - 1, 0)])


def _moe_up_kernel(te_ref, nu_ref, x_ref, wg_ref, wu_ref, o_ref, wgb_ref, wub_ref):
    t = pl.program_id(1)

    @pl.when(_expert_changed(te_ref, t))
    def _():
        wgb_ref[...] = wg_ref[...].astype(BF16)
        wub_ref[...] = wu_ref[...].astype(BF16)

    @pl.when(t < nu_ref[0])
    def _():
        x = x_ref[...]
        g = _dot(x, wgb_ref[...])
        u = _dot(x, wub_ref[...])
        o_ref[...] = (g * _sigmoid(g) * u).astype(o_ref.dtype)

    @pl.when(t >= nu_ref[0])
    def _():
        o_ref[...] = jnp.zeros(o_ref.shape, o_ref.dtype)


def _moe_up_call(te, nu, xs, w_gate, w_up, layer, tm, tf_pref=512):
    n_rows, d = xs.shape
    f = w_gate.shape[3]
    tf = _pick(f, tf_pref)
    return pl.pallas_call(
        _moe_up_kernel,
        grid_spec=pltpu.PrefetchScalarGridSpec(
            num_scalar_prefetch=2,
            grid=(f // tf, n_rows // tm),
            in_specs=[pl.BlockSpec((tm, d), lambda j, t, te, nu: (t, 0)),
                      pl.BlockSpec((None, None, d, tf), lambda j, t, te, nu: (layer, te[t], 0, j)),
                      pl.BlockSpec((None, None, d, tf), lambda j, t, te, nu: (layer, te[t], 0, j))],
            out_specs=pl.BlockSpec((tm, tf), lambda j, t, te, nu: (t, j)),
            scratch_shapes=[pltpu.VMEM((d, tf), BF16), pltpu.VMEM((d, tf), BF16)]),
        out_shape=jax.ShapeDtypeStruct((n_rows, f), BF16),
        compiler_params=_params(("arbitrary", "arbitrary")),
        name="moe_up",
    )(te, nu, xs, w_gate, w_up)


def _moe_down_kernel(te_ref, nu_ref, h_ref, wd_ref, o_ref, wdb_ref):
    t = pl.program_id(1)

    @pl.when(_expert_changed(te_ref, t))
    def _():
        wdb_ref[...] = wd_ref[...].astype(BF16)

    @pl.when(t < nu_ref[0])
    def _():
        o_ref[...] = _dot(h_ref[...], wdb_ref[...])

    @pl.when(t >= nu_ref[0])
    def _():
        o_ref[...] = jnp.zeros(o_ref.shape, o_ref.dtype)


def _moe_down_call(te, nu, hmid, w_down, layer, tm, tn_pref=2048):
    n_rows, f = hmid.shape
    d = w_down.shape[3]
    tn = _pick(d, tn_pref)
    return pl.pallas_call(
        _moe_down_kernel,
        grid_spec=pltpu.PrefetchScalarGridSpec(
            num_scalar_prefetch=2,
            grid=(d // tn, n_rows // tm),
            in_specs=[pl.BlockSpec((tm, f), lambda j, t, te, nu: (t, 0)),
                      pl.BlockSpec((None, None, f, tn), lambda j, t, te, nu: (layer, te[t], 0, j))],
            out_specs=pl.BlockSpec((tm, tn), lambda j, t, te, nu: (t, j)),
            scratch_shapes=[pltpu.VMEM((f, tn), BF16)]),
        out_shape=jax.ShapeDtypeStruct((n_rows, d), F32),
        compiler_params=_params(("arbitrary", "arbitrary")),
        name="moe_down",
    )(te, nu, hmid, w_down)


def _moe_combine_kernel(pos_ref, x_ref, wt_ref, gt_ref, g_ref, sc_ref, sh_ref, ys_hbm, *rest,
                        tm, final):
    if final:
        o_ref, buf_ref, sem = rest
    else:
        o_ref, h_ref, buf_ref, sem = rest
    base = pl.program_id(0) * tm

    def row_copy(r, kk):
        return pltpu.make_async_copy(ys_hbm.at[pl.ds(pos_ref[(base + r) * 2 + kk], 1), :],
                                     buf_ref.at[kk, pl.ds(r, 1), :], sem)

    def issue(r, carry):
        row_copy(r, 0).start()
        row_copy(r, 1).start()
        return carry

    def drain(r, carry):
        row_copy(r, 0).wait()
        row_copy(r, 1).wait()
        return carry

    lax.fori_loop(0, tm, issue, 0, unroll=4)
    lax.fori_loop(0, tm, drain, 0, unroll=4)
    wt = wt_ref[...]
    y = wt[:, 0:1] * buf_ref[0] + wt[:, 1:2] * buf_ref[1]
    xn = x_ref[...] + gt_ref[...] * y
    if final:
        ms = jnp.mean(xn * xn, axis=-1, keepdims=True)
        o_ref[...] = xn * lax.rsqrt(ms + _EPS) * g_ref[...]
    else:
        o_ref[...] = xn
        h_ref[...] = _norm_mod(xn, g_ref[...], sc_ref[...], sh_ref[...]).astype(h_ref.dtype)


def _moe_combine_call(pos, x, wt, modl, gate_idx, ys, seq, post_g, post_modl):
    t, d = x.shape
    tm = _pick(seq, 128)
    tpb = seq // tm
    final = post_modl is None
    if final:
        post_modl = modl
    row_spec = pl.BlockSpec((tm, d), lambda i, pos: (i, 0))
    out_specs = row_spec if final else [row_spec, row_spec]
    out_shape = (jax.ShapeDtypeStruct((t, d), F32) if final else
                 [jax.ShapeDtypeStruct((t, d), F32), jax.ShapeDtypeStruct((t, d), BF16)])
    return pl.pallas_call(
        functools.partial(_moe_combine_kernel, tm=tm, final=final),
        grid_spec=pltpu.PrefetchScalarGridSpec(
            num_scalar_prefetch=1,
            grid=(t // tm,),
            in_specs=[row_spec,
                      pl.BlockSpec((tm, 2), lambda i, pos: (i, 0)),
                      pl.BlockSpec((None, None, 1, d), lambda i, pos: (i // tpb, gate_idx, 0, 0)),
                      pl.BlockSpec((1, d), lambda i, pos: (0, 0)),
                      pl.BlockSpec((None, None, 1, d), lambda i, pos: (i // tpb, 1, 0, 0)),
                      pl.BlockSpec((None, None, 1, d), lambda i, pos: (i // tpb, 0, 0, 0)),
                      pl.BlockSpec(memory_space=pl.ANY)],
            out_specs=out_specs,
            scratch_shapes=[pltpu.VMEM((2, tm, d), F32), pltpu.SemaphoreType.DMA(())]),
        out_shape=out_shape,
        compiler_params=_params(("arbitrary",)),
        name="moe_combine",
    )(pos, x, wt, modl, post_g.astype(F32).reshape(1, d), post_modl, post_modl, ys)


def _route_tables(idx, n_exp, tm):
    n_sel = idx.shape[0] * idx.shape[1]
    e_flat = idx.reshape(-1)
    onehot = (e_flat[:, None] == jnp.arange(n_exp, dtype=jnp.int32)[None, :]).astype(jnp.int32)
    csum = jnp.cumsum(onehot, axis=0)
    counts = csum[-1]
    rank = jnp.sum(onehot * (csum - 1), axis=1)
    padded = ((counts + tm - 1) // tm) * tm
    ends = jnp.cumsum(padded)
    starts = ends - padded
    pos = jnp.sum(onehot * starts[None, :], axis=1) + rank
    n_rows = -(-n_sel // tm) * tm + n_exp * tm
    src = jnp.zeros((n_rows,), jnp.int32).at[pos].set(
        jnp.arange(n_sel, dtype=jnp.int32) // idx.shape[1])
    n_tiles = n_rows // tm
    n_used = (ends[-1] // tm).astype(jnp.int32)
    tile = jnp.minimum(jnp.arange(n_tiles, dtype=jnp.int32), n_used - 1)
    te = jnp.sum((tile[:, None] * tm >= ends[None, :]).astype(jnp.int32), axis=1)
    te = jnp.minimum(te, n_exp - 1).astype(jnp.int32)
    return pos.astype(jnp.int32), src, te, n_used.reshape(1), n_rows


def _moe_block(x, norm_g, modl, seq, router_w, router_bias, w_gate, w_up, w_down, layer,
               post_g, post_modl):
    n_exp = router_w.shape[1]
    h, idx_t, wt_t = _moe_norm_router_call(x, norm_g, modl, seq, router_w, router_bias)
    idx = idx_t[:2].T
    wt = wt_t[:2].T
    tm = 256
    pos, src, te, nu, n_rows = _route_tables(idx, n_exp, tm)
    xs = _gather_rows_call(src, nu, h, n_rows, tm)
    hmid = _moe_up_call(te, nu, xs, w_gate, w_up, layer, tm)
    ys = _moe_down_call(te, nu, hmid, w_down, layer, tm)
    return _moe_combine_call(pos, x, wt, modl, 5, ys, seq, post_g, post_modl)


def _gdn_mixer(x, h, modl, bsz, seq, w_in, conv_w, a_log, dt_bias, gnorm_g, w_out):
    d = x.shape[1]
    hv = a_log.shape[0]
    dh = gnorm_g.shape[0]
    conv_dim = conv_w.shape[1]
    value_dim = hv * dh
    key_dim = (conv_dim - value_dim) // 2
    hk = key_dim // dh
    assert w_in.shape[1] == conv_dim + value_dim + 2 * hv
    w_in_t = w_in.T
    proj = _mm_wt(h, w_in_t, conv_dim + value_dim, BF16)
    bg, bgt = _gdn_ba_call(h, w_in_t[conv_dim + value_dim:], a_log, dt_bias, hv)
    qkv = _dwconv_silu_call(proj, conv_dim, conv_w, seq)
    o = _delta_call(qkv, proj, bg, bgt.reshape(_LANES, 1, -1), gnorm_g, bsz, seq, hk, hv, dh)
    return _mm_resid(o, w_out, jnp.zeros((d,), F32), x, modl, 2, seq)


def _conformer_mixer(x, h, modl, seq, w_in, b_in, dw_w, dw_b, ln_g, ln_b, w_out, b_out):
    inner = w_in.shape[1] // 2
    u = _mm_glu(h, w_in, b_in, BF16)
    u = _dwconv_ln_call(u, dw_w, dw_b, ln_g, ln_b, seq)
    return _mm_resid(u, w_out, b_out, x, modl, 2, seq)


def kernel(x, c, ada_w, ada_b, norm_g, gdn_w_in, gdn_conv_w, gdn_a_log, gdn_dt_bias, gdn_norm_g, gdn_w_out, conf_w_in, conf_b_in, conf_dw_w, conf_dw_b, conf_ln_g, conf_ln_b, conf_w_out, conf_b_out, router_w, router_bias, moe_w_gate, moe_w_up, moe_w_down, final_norm_g):
    bsz, seq, d = x.shape
    depth = ada_w.shape[0]
    mod = _adaln(c, ada_w, ada_b)
    xf = x.reshape(bsz * seq, d)
    h = _norm_mod_call(xf, norm_g[0, 0], mod[0], 1, 0, seq, BF16)
    for i in range(depth):
        j = i // 2
        if i % 2 == 0:
            xf = _gdn_mixer(xf, h, mod[i], bsz, seq, gdn_w_in[j], gdn_conv_w[j],
                            gdn_a_log[j], gdn_dt_bias[j], gdn_norm_g[j], gdn_w_out[j])
        else:
            xf = _conformer_mixer(xf, h, mod[i], seq, conf_w_in[j], conf_b_in[j],
                                  conf_dw_w[j], conf_dw_b[j], conf_ln_g[j], conf_ln_b[j],
                                  conf_w_out[j], conf_b_out[j])
        last = i + 1 == depth
        out = _moe_block(xf, norm_g[i, 1], mod[i], seq, router_w, router_bias,
                         moe_w_gate, moe_w_up, moe_w_down, i,
                         final_norm_g if last else norm_g[i + 1, 0],
                         None if last else mod[i + 1])
        if last:
            return out.reshape(bsz, seq, d)
        xf, h = out
```

```python
import functools

import jax
import jax.numpy as jnp
from jax import lax
from jax.experimental import pallas as pl
from jax.experimental.pallas import tpu as pltpu

F32 = jnp.float32
BF16 = jnp.bfloat16

_EPS = 1e-6
_CHUNK = 64
_N_GROUPS = 4
_N_MOD = 6
_LANES = 128
_VMEM_LIMIT = 56 * 1024 * 1024


def _pick(dim, pref):
    t = min(pref, dim)
    while dim % t:
        t //= 2
    return max(t, 1)


def _params(sem, vmem=_VMEM_LIMIT):
    return pltpu.CompilerParams(dimension_semantics=sem, vmem_limit_bytes=vmem)


def _sigmoid(x):
    return 1.0 / (1.0 + jnp.exp(-x))


def _dot(a, b):
    return jnp.dot(a, b, preferred_element_type=F32)


def _dot_nt(a, b):
    return lax.dot_general(a, b, (((1,), (1,)), ((), ())), preferred_element_type=F32)


def _dot_tn(a, b):
    return lax.dot_general(a, b, (((0,), (0,)), ((), ())), preferred_element_type=F32)


def _adaln_kernel(c_ref, w_ref, b_ref, o_ref):
    c = c_ref[...]
    ca = (c * _sigmoid(c)).astype(BF16)
    o_ref[...] = _dot(ca, w_ref[...].astype(BF16)) + b_ref[...]


def _adaln(c, ada_w, ada_b):
    depth, d, n = ada_w.shape
    bsz = c.shape[0]
    bp = -(-bsz // 8) * 8
    c8 = jnp.pad(c, ((0, bp - bsz), (0, 0)))
    tn = _pick(n, 512)
    out = pl.pallas_call(
        _adaln_kernel,
        grid=(depth, n // tn),
        in_specs=[pl.BlockSpec((bp, d), lambda l, j: (0, 0)),
                  pl.BlockSpec((None, d, tn), lambda l, j: (l, 0, j)),
                  pl.BlockSpec((None, 1, tn), lambda l, j: (l, 0, j))],
        out_specs=pl.BlockSpec((None, bp, tn), lambda l, j: (l, 0, j)),
        out_shape=jax.ShapeDtypeStruct((depth, bp, n), F32),
        compiler_params=_params(("arbitrary", "arbitrary")),
        name="adaln",
    )(c8, ada_w, ada_b.reshape(depth, 1, n))
    return out[:, :bsz].reshape(depth, bsz, _N_MOD, 1, d)


def _norm_mod(x, g, sc, sh):
    ms = jnp.mean(x * x, axis=-1, keepdims=True)
    return x * lax.rsqrt(ms + _EPS) * g * (1.0 + sc) + sh


def _norm_mod_kernel(x_ref, g_ref, sc_ref, sh_ref, o_ref):
    o_ref[...] = _norm_mod(x_ref[...], g_ref[...], sc_ref[...], sh_ref[...]).astype(o_ref.dtype)


def _norm_mod_call(x, g, modl, sc_idx, sh_idx, seq, out_dtype):
    t, d = x.shape
    tm = _pick(seq, 256)
    tpb = seq // tm
    return pl.pallas_call(
        _norm_mod_kernel,
        grid=(t // tm,),
        in_specs=[pl.BlockSpec((tm, d), lambda i: (i, 0)),
                  pl.BlockSpec((1, d), lambda i: (0, 0)),
                  pl.BlockSpec((None, None, 1, d), lambda i: (i // tpb, sc_idx, 0, 0)),
                  pl.BlockSpec((None, None, 1, d), lambda i: (i // tpb, sh_idx, 0, 0))],
        out_specs=pl.BlockSpec((tm, d), lambda i: (i, 0)),
        out_shape=jax.ShapeDtypeStruct((t, d), out_dtype),
        compiler_params=_params(("arbitrary",)),
        name="norm_mod",
    )(x, g.reshape(1, d), modl, modl)


def _top2_sum(v):
    a = jnp.maximum(v[0], v[1]); b = jnp.minimum(v[0], v[1])
    c = jnp.maximum(v[2], v[3]); d = jnp.minimum(v[2], v[3])
    return jnp.maximum(a, c) + jnp.maximum(jnp.minimum(a, c), jnp.maximum(b, d))


def _first_argmax(vals):
    best = vals[0]
    idx = jnp.zeros(best.shape, jnp.int32)
    for e in range(1, len(vals)):
        take = vals[e] > best
        idx = jnp.where(take, e, idx)
        best = jnp.where(take, vals[e], best)
    return idx, best


def _moe_norm_router_kernel(x_ref, g_ref, sc_ref, sh_ref, rwh_ref, rwl_ref, rb_ref,
                            h_ref, idx_ref, wt_ref, *, n_exp):
    h = _norm_mod(x_ref[...], g_ref[...], sc_ref[...], sh_ref[...])
    h_ref[...] = h
    h_hi = h.astype(BF16)
    h_lo = (h - h_hi.astype(F32)).astype(BF16)
    logits = (_dot(h_hi, rwh_ref[...])
              + (_dot(h_lo, rwh_ref[...]) + _dot(h_hi, rwl_ref[...])))
    lt = logits.T
    per_group = n_exp // _N_GROUPS
    aff = [_sigmoid(lt[e:e + 1, :]) for e in range(n_exp)]
    biased = [aff[e] + rb_ref[e] for e in range(n_exp)]
    gscore = [_top2_sum(biased[gi * per_group:(gi + 1) * per_group]) for gi in range(_N_GROUPS)]
    best_group, _ = _first_argmax(gscore)
    neg = jnp.full(aff[0].shape, -jnp.inf, F32)
    masked = [jnp.where(best_group == (e // per_group), biased[e], neg) for e in range(n_exp)]
    i1, _ = _first_argmax(masked)
    masked2 = [jnp.where(i1 == e, neg, masked[e]) for e in range(n_exp)]
    i2, _ = _first_argmax(masked2)
    zero = jnp.zeros(aff[0].shape, F32)
    a1 = zero
    a2 = zero
    for e in range(n_exp):
        a1 = a1 + jnp.where(i1 == e, aff[e], zero)
        a2 = a2 + jnp.where(i2 == e, aff[e], zero)
    den = a1 + a2
    idx_ref[...] = jnp.zeros(idx_ref.shape, jnp.int32)
    wt_ref[...] = jnp.zeros(wt_ref.shape, F32)
    idx_ref[0:1, :] = i1
    idx_ref[1:2, :] = i2
    wt_ref[0:1, :] = a1 / den
    wt_ref[1:2, :] = a2 / den


def _moe_norm_router_call(x, g, modl, seq, router_w, router_bias):
    t, d = x.shape
    n_exp = router_w.shape[1]
    assert n_exp % _N_GROUPS == 0 and n_exp // _N_GROUPS == 4 and n_exp <= _LANES
    tm = _pick(seq, 256)
    tpb = seq // tm
    rw = jnp.pad(router_w.astype(F32), ((0, 0), (0, _LANES - n_exp)))
    rw_hi = rw.astype(BF16)
    rw_lo = (rw - rw_hi.astype(F32)).astype(BF16)
    return pl.pallas_call(
        functools.partial(_moe_norm_router_kernel, n_exp=n_exp),
        grid=(t // tm,),
        in_specs=[pl.BlockSpec((tm, d), lambda i: (i, 0)),
                  pl.BlockSpec((1, d), lambda i: (0, 0)),
                  pl.BlockSpec((None, None, 1, d), lambda i: (i // tpb, 4, 0, 0)),
                  pl.BlockSpec((None, None, 1, d), lambda i: (i // tpb, 3, 0, 0)),
                  pl.BlockSpec((d, _LANES), lambda i: (0, 0)),
                  pl.BlockSpec((d, _LANES), lambda i: (0, 0)),
                  pl.BlockSpec(memory_space=pltpu.SMEM)],
        out_specs=[pl.BlockSpec((tm, d), lambda i: (i, 0)),
                   pl.BlockSpec((8, tm), lambda i: (0, i)),
                   pl.BlockSpec((8, tm), lambda i: (0, i))],
        out_shape=[jax.ShapeDtypeStruct((t, d), F32),
                   jax.ShapeDtypeStruct((8, t), jnp.int32),
                   jax.ShapeDtypeStruct((8, t), F32)],
        compiler_params=_params(("arbitrary",)),
        name="moe_norm_router",
    )(x, g.reshape(1, d), modl, modl, rw_hi, rw_lo, router_bias.astype(F32))


def _mm_wt_kernel(x_ref, wt_ref, o_ref, wb_ref):
    @pl.when(pl.program_id(1) == 0)
    def _():
        wb_ref[...] = wt_ref[...].T.astype(BF16)
    o_ref[...] = _dot(x_ref[...], wb_ref[...]).astype(o_ref.dtype)


def _mm_wt(x, wt, n_out, out_dtype, tm_pref=1024, tn_pref=512):
    m, k = x.shape
    tm = _pick(m, tm_pref)
    tn = _pick(n_out, tn_pref)
    return pl.pallas_call(
        _mm_wt_kernel,
        grid=(n_out // tn, m // tm),
        in_specs=[pl.BlockSpec((tm, k), lambda j, i: (i, 0)),
                  pl.BlockSpec((tn, k), lambda j, i: (j, 0))],
        out_specs=pl.BlockSpec((tm, tn), lambda j, i: (i, j)),
        out_shape=jax.ShapeDtypeStruct((m, n_out), out_dtype),
        scratch_shapes=[pltpu.VMEM((k, tn), BF16)],
        compiler_params=_params(("arbitrary", "arbitrary")),
        name="mm_wt",
    )(x, wt)


def _mm_resid_kernel(x_ref, w_ref, b_ref, res_ref, gt_ref, o_ref, wb_ref):
    @pl.when(pl.program_id(1) == 0)
    def _():
        wb_ref[...] = w_ref[...].astype(BF16)
    y = _dot(x_ref[...], wb_ref[...]) + b_ref[...]
    o_ref[...] = res_ref[...] + gt_ref[...] * y


def _mm_resid(x, w, bias, res, modl, gate_idx, seq, tm_pref=1024, tn_pref=512):
    m, k = x.shape
    n = w.shape[1]
    tm = _pick(seq, tm_pref)
    tn = _pick(n, tn_pref)
    tpb = seq // tm
    return pl.pallas_call(
        _mm_resid_kernel,
        grid=(n // tn, m // tm),
        in_specs=[pl.BlockSpec((tm, k), lambda j, i: (i, 0)),
                  pl.BlockSpec((k, tn), lambda j, i: (0, j)),
                  pl.BlockSpec((1, tn), lambda j, i: (0, j)),
                  pl.BlockSpec((tm, tn), lambda j, i: (i, j)),
                  pl.BlockSpec((None, None, 1, tn), lambda j, i: (i // tpb, gate_idx, 0, j))],
        out_specs=pl.BlockSpec((tm, tn), lambda j, i: (i, j)),
        out_shape=jax.ShapeDtypeStruct((m, n), F32),
        scratch_shapes=[pltpu.VMEM((k, tn), BF16)],
        compiler_params=_params(("arbitrary", "arbitrary")),
        name="mm_resid",
    )(x, w, bias.reshape(1, n), res, modl)


def _mm_glu_kernel(x_ref, wv_ref, wg_ref, bv_ref, bg_ref, o_ref, wvb_ref, wgb_ref):
    @pl.when(pl.program_id(1) == 0)
    def _():
        wvb_ref[...] = wv_ref[...].astype(BF16)
        wgb_ref[...] = wg_ref[...].astype(BF16)
    x = x_ref[...]
    val = _dot(x, wvb_ref[...]) + bv_ref[...]
    gate = _dot(x, wgb_ref[...]) + bg_ref[...]
    o_ref[...] = (val * _sigmoid(gate)).astype(o_ref.dtype)


def _mm_glu(x, w, b, out_dtype, tm_pref=1024, tn_pref=256):
    m, k = x.shape
    n = w.shape[1] // 2
    tm = _pick(m, tm_pref)
    tn = _pick(n, tn_pref)
    nb = n // tn
    b2 = b.reshape(1, 2 * n)
    return pl.pallas_call(
        _mm_glu_kernel,
        grid=(nb, m // tm),
        in_specs=[pl.BlockSpec((tm, k), lambda j, i: (i, 0)),
                  pl.BlockSpec((k, tn), lambda j, i: (0, j)),
                  pl.BlockSpec((k, tn), lambda j, i: (0, j + nb)),
                  pl.BlockSpec((1, tn), lambda j, i: (0, j)),
                  pl.BlockSpec((1, tn), lambda j, i: (0, j + nb))],
        out_specs=pl.BlockSpec((tm, tn), lambda j, i: (i, j)),
        out_shape=jax.ShapeDtypeStruct((m, n), out_dtype),
        scratch_shapes=[pltpu.VMEM((k, tn), BF16), pltpu.VMEM((k, tn), BF16)],
        compiler_params=_params(("arbitrary", "arbitrary")),
        name="mm_glu",
    )(x, w, w, b2, b2)


def _gdn_ba_kernel(h_ref, w_ref, al_ref, dt_ref, o_ref, ot_ref, *, hv):
    p = _dot_nt(h_ref[...], w_ref[...].astype(BF16))
    tm = p.shape[0]
    lane = lax.broadcasted_iota(jnp.int32, p.shape, 1)
    beta = _sigmoid(p)
    a = p + dt_ref[...]
    softplus = jnp.maximum(a, 0.0) + jnp.log1p(jnp.exp(-jnp.abs(a)))
    g = -jnp.exp(al_ref[...]) * softplus
    ri = lax.broadcasted_iota(jnp.int32, (_CHUNK, _CHUNK), 0)
    ci = lax.broadcasted_iota(jnp.int32, (_CHUNK, _CHUNK), 1)
    tril = (ri >= ci).astype(F32)
    parts = [jnp.dot(tril, g[c * _CHUNK:(c + 1) * _CHUNK], preferred_element_type=F32,
                     precision=lax.Precision.HIGHEST) for c in range(tm // _CHUNK)]
    gcum = jnp.concatenate(parts, axis=0)
    out = jnp.where(lane < hv, beta, jnp.where(lane < 2 * hv, gcum, 0.0))
    o_ref[...] = out
    ot_ref[...] = out.T


def _gdn_ba_call(h, w_ba_t, a_log, dt_bias, hv):
    t, d = h.shape
    assert 2 * hv <= _LANES
    pad = _LANES - 2 * hv
    w128 = jnp.pad(w_ba_t, ((0, pad), (0, 0)))
    al = jnp.pad(a_log.astype(F32), (hv, pad)).reshape(1, _LANES)
    dt = jnp.pad(dt_bias.astype(F32), (hv, pad)).reshape(1, _LANES)
    tm = _pick(t, 512)
    assert tm % _CHUNK == 0
    return pl.pallas_call(
        functools.partial(_gdn_ba_kernel, hv=hv),
        grid=(t // tm,),
        in_specs=[pl.BlockSpec((tm, d), lambda i: (i, 0)),
                  pl.BlockSpec((_LANES, d), lambda i: (0, 0)),
                  pl.BlockSpec((1, _LANES), lambda i: (0, 0)),
                  pl.BlockSpec((1, _LANES), lambda i: (0, 0))],
        out_specs=[pl.BlockSpec((tm, _LANES), lambda i: (i, 0)),
                   pl.BlockSpec((_LANES, tm), lambda i: (0, i))],
        out_shape=[jax.ShapeDtypeStruct((t, _LANES), F32),
                   jax.ShapeDtypeStruct((_LANES, t), F32)],
        compiler_params=_params(("arbitrary",)),
        name="gdn_ba",
    )(h, w128, al, dt)


def _conv_halo_rows(width):
    return 16 * (-(-(width - 1) // 16))


def _fill_conv_buffer(buf_ref, cur_ref, halo_ref, halo, tpb):
    ts = cur_ref.shape[0]
    first = (pl.program_id(0) % tpb) == 0
    buf_ref[0:halo, :] = jnp.where(first, 0.0, halo_ref[...].astype(F32))
    buf_ref[halo:halo + ts, :] = cur_ref[...].astype(F32)


def _dwconv_silu_kernel(cur_ref, halo_ref, w_ref, o_ref, buf_ref, *, width, halo, tpb, rows):
    ts = cur_ref.shape[0]
    _fill_conv_buffer(buf_ref, cur_ref, halo_ref, halo, tpb)
    base = halo - (width - 1)
    for r0 in range(0, ts, rows):
        acc = None
        for kk in range(width):
            term = buf_ref[base + r0 + kk:base + r0 + kk + rows, :] * w_ref[kk:kk + 1, :]
            acc = term if acc is None else acc + term
        o_ref[r0:r0 + rows, :] = (acc * _sigmoid(acc)).astype(o_ref.dtype)


def _dwconv_silu_call(x, n_ch, w, seq, ts_pref=512, tc_pref=512, rows=64):
    t = x.shape[0]
    width = w.shape[0]
    halo = _conv_halo_rows(width)
    ts = _pick(seq, ts_pref)
    tc = _pick(n_ch, tc_pref)
    rows = min(rows, ts)
    assert ts % halo == 0 and ts % rows == 0
    tpb = seq // ts
    hb = ts // halo
    return pl.pallas_call(
        functools.partial(_dwconv_silu_kernel, width=width, halo=halo, tpb=tpb, rows=rows),
        grid=(t // ts, n_ch // tc),
        in_specs=[pl.BlockSpec((ts, tc), lambda i, j: (i, j)),
                  pl.BlockSpec((halo, tc), lambda i, j: (jnp.maximum(i * hb - 1, 0), j)),
                  pl.BlockSpec((width, tc), lambda i, j: (0, j))],
        out_specs=pl.BlockSpec((ts, tc), lambda i, j: (i, j)),
        out_shape=jax.ShapeDtypeStruct((t, n_ch), BF16),
        scratch_shapes=[pltpu.VMEM((halo + ts, tc), F32)],
        compiler_params=_params(("arbitrary", "arbitrary")),
        name="dwconv_silu",
    )(x, x, w.astype(F32))


def _dwconv_ln_kernel(cur_ref, halo_ref, w_ref, b_ref, lg_ref, lb_ref, o_ref, buf_ref, sh_ref,
                      acc_ref, *, width, halo, tpb, rows, cw):
    ts, nch = cur_ref.shape
    _fill_conv_buffer(buf_ref, cur_ref, halo_ref, halo, tpb)
    base = halo - (width - 1)
    n_cc = nch // cw
    n_sh = halo + ts - 8

    def shift_block(cc, carry):
        c0 = pl.multiple_of(cc * cw, cw)
        blk = buf_ref[:, pl.ds(c0, cw)]
        for s in range(1, 8):
            sh_ref[s - 1, :, pl.ds(c0, cw)] = blk[s:s + n_sh]
        return carry

    lax.fori_loop(0, n_cc, shift_block, 0)

    def conv_block(it, carry):
        r0 = pl.multiple_of((it // n_cc) * rows, rows)
        c0 = pl.multiple_of((it % n_cc) * cw, cw)
        acc = None
        for kk in range(width):
            a, s = divmod(base + kk, 8)
            if s == 0:
                win = buf_ref[pl.ds(r0 + 8 * a, rows), pl.ds(c0, cw)]
            else:
                win = sh_ref[s - 1, pl.ds(r0 + 8 * a, rows), pl.ds(c0, cw)]
            term = win * w_ref[kk:kk + 1, pl.ds(c0, cw)]
            acc = term if acc is None else acc + term
        acc_ref[pl.ds(r0, rows), pl.ds(c0, cw)] = acc + b_ref[:, pl.ds(c0, cw)]
        return carry

    lax.fori_loop(0, (ts // rows) * n_cc, conv_block, 0)

    ln_rows = 16

    def norm_block(it, carry):
        r0 = pl.multiple_of(it * ln_rows, ln_rows)
        rows_ref = acc_ref.at[pl.ds(r0, ln_rows), :]
        mu = jnp.mean(rows_ref[...], axis=-1, keepdims=True)
        var = jnp.mean(jnp.square(rows_ref[...] - mu), axis=-1, keepdims=True)
        scale = lax.rsqrt(var + _EPS)
        for c0 in range(0, nch, cw):
            y = ((rows_ref[:, c0:c0 + cw] - mu) * scale * lg_ref[:, c0:c0 + cw]
                 + lb_ref[:, c0:c0 + cw])
            o_ref[pl.ds(r0, ln_rows), c0:c0 + cw] = (y * _sigmoid(y)).astype(o_ref.dtype)
        return carry

    lax.fori_loop(0, ts // ln_rows, norm_block, 0, unroll=4)


def _dwconv_ln_call(x, w, bias, ln_g, ln_b, seq, ts_pref=128, rows=128, cw=256):
    t, n_ch = x.shape
    width = w.shape[0]
    halo = _conv_halo_rows(width)
    ts = _pick(seq, ts_pref)
    cw = _pick(n_ch, cw)
    rows = min(rows, ts)
    assert ts % halo == 0 and ts % rows == 0 and ts % 16 == 0
    tpb = seq // ts
    hb = ts // halo

    def row(v):
        return v.astype(F32).reshape(1, n_ch)

    return pl.pallas_call(
        functools.partial(_dwconv_ln_kernel, width=width, halo=halo, tpb=tpb, rows=rows, cw=cw),
        grid=(t // ts,),
        in_specs=[pl.BlockSpec((ts, n_ch), lambda i: (i, 0)),
                  pl.BlockSpec((halo, n_ch), lambda i: (jnp.maximum(i * hb - 1, 0), 0)),
                  pl.BlockSpec((width, n_ch), lambda i: (0, 0)),
                  pl.BlockSpec((1, n_ch), lambda i: (0, 0)),
                  pl.BlockSpec((1, n_ch), lambda i: (0, 0)),
                  pl.BlockSpec((1, n_ch), lambda i: (0, 0))],
        out_specs=pl.BlockSpec((ts, n_ch), lambda i: (i, 0)),
        out_shape=jax.ShapeDtypeStruct((t, n_ch), BF16),
        scratch_shapes=[pltpu.VMEM((halo + ts, n_ch), F32),
                        pltpu.VMEM((7, halo + ts - 8, n_ch), F32),
                        pltpu.VMEM((ts, n_ch), F32)],
        compiler_params=_params(("arbitrary",)),
        name="dwconv_ln",
    )(x, x, w.astype(F32), row(bias), row(ln_g), row(ln_b))


def _delta_kernel(q_ref, k_ref, v_ref, z_ref, bg_ref, grow_ref, ng_ref, o_ref, s_ref,
                  *, hb, rep, dh, hv_total, scale):
    @pl.when(pl.program_id(2) == 0)
    def _():
        s_ref[...] = jnp.zeros(s_ref.shape, F32)

    ts = q_ref.shape[0]
    nc = ts // _CHUNK
    bg = bg_ref[...]
    lane = lax.broadcasted_iota(jnp.int32, bg.shape, 1)
    ri = lax.broadcasted_iota(jnp.int32, (ts, ts), 0)
    ci = lax.broadcasted_iota(jnp.int32, (ts, ts), 1)
    same = (ri // _CHUNK) == (ci // _CHUNK)
    causal = jnp.logical_and(same, ri >= ci)
    strict = jnp.logical_and(same, ri > ci)
    ng = ng_ref[...]
    n_fac = _CHUNK.bit_length() - 1
    heads = [(hl, r) for hl in range(hb) for r in range(rep)]
    nh = len(heads)

    qn_l, kn_l, qk_l, kk_l = [], [], [], []
    for hl in range(hb):
        q = q_ref[:, hl * dh:(hl + 1) * dh].astype(F32)
        k = k_ref[:, hl * dh:(hl + 1) * dh].astype(F32)
        qn = q * lax.rsqrt(jnp.sum(q * q, axis=-1, keepdims=True) + _EPS) * scale
        kn = k * lax.rsqrt(jnp.sum(k * k, axis=-1, keepdims=True) + _EPS)
        kb = kn.astype(BF16)
        qkk = _dot_nt(jnp.concatenate([qn.astype(BF16), kb], axis=0), kb)
        qn_l.append(qn)
        kn_l.append(kn)
        qk_l.append(qkk[:ts])
        kk_l.append(qkk[ts:])

    gcol_l, decay_l, eg_l, q_pow, r_acc = [], [], [], [], []
    for hi, (hl, r) in enumerate(heads):
        hv = (pl.program_id(1) * hb + hl) * rep + r
        bcol = jnp.sum(jnp.where(lane == hv, bg, 0.0), axis=-1, keepdims=True)
        gcol = jnp.sum(jnp.where(lane == hv_total + hv, bg, 0.0), axis=-1, keepdims=True)
        grow = grow_ref[hi]
        v = v_ref[:, hi * dh:(hi + 1) * dh].astype(F32)
        decay = jnp.where(causal, jnp.exp(jnp.where(causal, gcol - grow, 0.0)), 0.0)
        eg = jnp.exp(gcol)
        gcol_l.append(gcol)
        decay_l.append(decay)
        eg_l.append(eg)
        q_pow.append(-jnp.where(strict, kk_l[hl] * bcol * decay, 0.0))
        r_acc.append(jnp.concatenate([v * bcol, kn_l[hl] * (bcol * eg)], axis=1))

    eye = (ri == ci).astype(F32)
    t_inv = [eye + q_pow[hi] for hi in range(nh)]
    for hi in range(nh):
        qb = q_pow[hi].astype(BF16)
        q_pow[hi] = _dot(qb, qb)
    for i in range(1, n_fac):
        for hi in range(nh):
            qb = q_pow[hi].astype(BF16)
            pb = t_inv[hi].astype(BF16)
            if i + 1 < n_fac:
                y = _dot(jnp.concatenate([pb, qb], axis=0), qb)
                t_inv[hi] = t_inv[hi] + y[:ts]
                q_pow[hi] = y[ts:]
            else:
                t_inv[hi] = t_inv[hi] + _dot(pb, qb)
    for hi in range(nh):
        r_acc[hi] = _dot(t_inv[hi].astype(BF16), r_acc[hi].astype(BF16))

    sol_l, qeff_l, oloc_l, kdec_l = [], [], [], []
    for hi, (hl, r) in enumerate(heads):
        sol = r_acc[hi].astype(BF16)
        qx = _dot((qk_l[hl] * decay_l[hi]).astype(BF16), sol)
        gcol = gcol_l[hi]
        glast = jnp.concatenate(
            [jnp.broadcast_to(gcol[(c + 1) * _CHUNK - 1:(c + 1) * _CHUNK, :], (_CHUNK, 1))
             for c in range(nc)], axis=0)
        sol_l.append(sol)
        qeff_l.append((qn_l[hl] * eg_l[hi] - qx[:, dh:]).astype(BF16))
        oloc_l.append(qx[:, :dh])
        kdec_l.append((kn_l[hl] * jnp.exp(glast - gcol)).astype(BF16))

    states = [s_ref[hi] for hi in range(nh)]
    for c in range(nc):
        sl = slice(c * _CHUNK, (c + 1) * _CHUNK)
        for hi in range(nh):
            kx = _dot_tn(kdec_l[hi][sl], sol_l[hi][sl])
            sb = states[hi].astype(BF16)
            y = _dot(jnp.concatenate([qeff_l[hi][sl], kx[:, dh:].astype(BF16)], axis=0), sb)
            o = y[:_CHUNK] + oloc_l[hi][sl]
            gl = jnp.exp(gcol_l[hi][(c + 1) * _CHUNK - 1:(c + 1) * _CHUNK, :])
            states[hi] = gl * states[hi] - y[_CHUNK:] + kx[:, :dh]
            on = o * lax.rsqrt(jnp.mean(o * o, axis=-1, keepdims=True) + _EPS) * ng
            zc = z_ref[sl, hi * dh:(hi + 1) * dh].astype(F32)
            o_ref[sl, hi * dh:(hi + 1) * dh] = (on * (zc * _sigmoid(zc))).astype(o_ref.dtype)
    for hi in range(nh):
        s_ref[hi] = states[hi]


def _delta_call(qkv, proj, bg, bgt3, norm_g, bsz, seq, hk, hv, dh, hb=4, ts_pref=128):
    t = qkv.shape[0]
    rep = hv // hk
    hb = min(hb, hk)
    assert hv % hk == 0 and hk % hb == 0 and (2 * hk) % (hb * rep) == 0 and dh % _LANES == 0
    conv_dim = qkv.shape[1]
    ts = _pick(seq, ts_pref)
    assert ts % _CHUNK == 0
    ns = seq // ts
    nh = hb * rep
    v_blk0 = (2 * hk) // nh
    z_blk0 = conv_dim // (nh * dh)
    return pl.pallas_call(
        functools.partial(_delta_kernel, hb=hb, rep=rep, dh=dh, hv_total=hv,
                          scale=float(dh) ** -0.5),
        grid=(bsz, hk // hb, ns),
        in_specs=[pl.BlockSpec((ts, hb * dh), lambda b, h, s: (b * ns + s, h)),
                  pl.BlockSpec((ts, hb * dh), lambda b, h, s: (b * ns + s, hk // hb + h)),
                  pl.BlockSpec((ts, nh * dh), lambda b, h, s: (b * ns + s, v_blk0 + h)),
                  pl.BlockSpec((ts, nh * dh), lambda b, h, s: (b * ns + s, z_blk0 + h)),
                  pl.BlockSpec((ts, _LANES), lambda b, h, s: (b * ns + s, 0)),
                  pl.BlockSpec((nh, 1, ts), lambda b, h, s: (hv // nh + h, 0, b * ns + s)),
                  pl.BlockSpec((1, dh), lambda b, h, s: (0, 0))],
        out_specs=pl.BlockSpec((ts, nh * dh), lambda b, h, s: (b * ns + s, h)),
        out_shape=jax.ShapeDtypeStruct((t, hv * dh), BF16),
        scratch_shapes=[pltpu.VMEM((nh, dh, dh), F32)],
        compiler_params=_params(("arbitrary", "arbitrary", "arbitrary")),
        name="gdn_delta",
    )(qkv, qkv, qkv, proj, bg, bgt3, norm_g.astype(F32).reshape(1, dh))


def _gather_rows_kernel(src_ref, nu_ref, x_hbm, o_ref, buf_ref, sem_ref, *, tm):
    t = pl.program_id(0)
    n_used = nu_ref[0]
    slot = t % 2

    def row_copy(tile, slot_, r):
        return pltpu.make_async_copy(x_hbm.at[pl.ds(src_ref[tile * tm + r], 1), :],
                                     buf_ref.at[slot_, pl.ds(r, 1), :], sem_ref.at[slot_])

    def issue(tile, slot_):
        def body(r, carry):
            row_copy(tile, slot_, r).start()
            return carry
        lax.fori_loop(0, tm, body, 0, unroll=8)

    @pl.when(jnp.logical_and(t == 0, n_used > 0))
    def _():
        issue(0, 0)

    @pl.when(t + 1 < n_used)
    def _():
        issue(t + 1, 1 - slot)

    @pl.when(t < n_used)
    def _():
        def body(r, carry):
            row_copy(t, slot, r).wait()
            return carry
        lax.fori_loop(0, tm, body, 0, unroll=8)
        o_ref[...] = buf_ref[slot].astype(o_ref.dtype)

    @pl.when(t >= n_used)
    def _():
        o_ref[...] = jnp.zeros(o_ref.shape, o_ref.dtype)


def _gather_rows_call(src, nu, x, n_rows, tm):
    d = x.shape[1]
    return pl.pallas_call(
        functools.partial(_gather_rows_kernel, tm=tm),
        grid_spec=pltpu.PrefetchScalarGridSpec(
            num_scalar_prefetch=2,
            grid=(n_rows // tm,),
            in_specs=[pl.BlockSpec(memory_space=pl.ANY)],
            out_specs=pl.BlockSpec((tm, d), lambda t, src, nu: (t, 0)),
            scratch_shapes=[pltpu.VMEM((2, tm, d), x.dtype), pltpu.SemaphoreType.DMA((2,))]),
        out_shape=jax.ShapeDtypeStruct((n_rows, d), BF16),
        compiler_params=_params(("arbitrary",)),
        name="moe_gather",
    )(src, nu, x)


def _expert_changed(te_ref, t):
    return jnp.logical_or(t == 0, te_ref[t] != te_ref[jnp.maximum(t - 1, 0)])


def _moe_up_kernel(te_ref, nu_ref, x_ref, wg_ref, wu_ref, o_ref, wgb_ref, wub_ref):
    t = pl.program_id(1)

    @pl.when(_expert_changed(te_ref, t))
    def _():
        wgb_ref[...] = wg_ref[...].astype(BF16)
        wub_ref[...] = wu_ref[...].astype(BF16)

    @pl.when(t < nu_ref[0])
    def _():
        x = x_ref[...]
        g = _dot(x, wgb_ref[...])
        u = _dot(x, wub_ref[...])
        o_ref[...] = (g * _sigmoid(g) * u).astype(o_ref.dtype)

    @pl.when(t >= nu_ref[0])
    def _():
        o_ref[...] = jnp.zeros(o_ref.shape, o_ref.dtype)


def _moe_up_call(te, nu, xs, w_gate, w_up, layer, tm, tf_pref=512):
    n_rows, d = xs.shape
    f = w_gate.shape[3]
    tf = _pick(f, tf_pref)
    return pl.pallas_call(
        _moe_up_kernel,
        grid_spec=pltpu.PrefetchScalarGridSpec(
            num_scalar_prefetch=2,
            grid=(f // tf, n_rows // tm),
            in_specs=[pl.BlockSpec((tm, d), lambda j, t, te, nu: (t, 0)),
                      pl.BlockSpec((None, None, d, tf), lambda j, t, te, nu: (layer, te[t], 0, j)),
                      pl.BlockSpec((None, None, d, tf), lambda j, t, te, nu: (layer, te[t], 0, j))],
            out_specs=pl.BlockSpec((tm, tf), lambda j, t, te, nu: (t, j)),
            scratch_shapes=[pltpu.VMEM((d, tf), BF16), pltpu.VMEM((d, tf), BF16)]),
        out_shape=jax.ShapeDtypeStruct((n_rows, f), BF16),
        compiler_params=_params(("arbitrary", "arbitrary")),
        name="moe_up",
    )(te, nu, xs, w_gate, w_up)


def _moe_down_kernel(te_ref, nu_ref, h_ref, wd_ref, o_ref, wdb_ref):
    t = pl.program_id(1)

    @pl.when(_expert_changed(te_ref, t))
    def _():
        wdb_ref[...] = wd_ref[...].astype(BF16)

    @pl.when(t < nu_ref[0])
    def _():
        o_ref[...] = _dot(h_ref[...], wdb_ref[...])

    @pl.when(t >= nu_ref[0])
    def _():
        o_ref[...] = jnp.zeros(o_ref.shape, o_ref.dtype)


def _moe_down_call(te, nu, hmid, w_down, layer, tm, tn_pref=2048):
    n_rows, f = hmid.shape
    d = w_down.shape[3]
    tn = _pick(d, tn_pref)
    return pl.pallas_call(
        _moe_down_kernel,
        grid_spec=pltpu.PrefetchScalarGridSpec(
            num_scalar_prefetch=2,
            grid=(d // tn, n_rows // tm),
            in_specs=[pl.BlockSpec((tm, f), lambda j, t, te, nu: (t, 0)),
                      pl.BlockSpec((None, None, f, tn), lambda j, t, te, nu: (layer, te[t], 0, j))],
            out_specs=pl.BlockSpec((tm, tn), lambda j, t, te, nu: (t, j)),
            scratch_shapes=[pltpu.VMEM((f, tn), BF16)]),
        out_shape=jax.ShapeDtypeStruct((n_rows, d), F32),
        compiler_params=_params(("arbitrary", "arbitrary")),
        name="moe_down",
    )(te, nu, hmid, w_down)


def _moe_combine_kernel(pos_ref, x_ref, wt_ref, gt_ref, g_ref, sc_ref, sh_ref, ys_hbm, *rest,
                        tm, final):
    if final:
        o_ref, buf_ref, sem = rest
    else:
        o_ref, h_ref, buf_ref, sem = rest
    base = pl.program_id(0) * tm

    def row_copy(r, kk):
        return pltpu.make_async_copy(ys_hbm.at[pl.ds(pos_ref[(base + r) * 2 + kk], 1), :],
                                     buf_ref.at[kk, pl.ds(r, 1), :], sem)

    def issue(r, carry):
        row_copy(r, 0).start()
        row_copy(r, 1).start()
        return carry

    def drain(r, carry):
        row_copy(r, 0).wait()
        row_copy(r, 1).wait()
        return carry

    lax.fori_loop(0, tm, issue, 0, unroll=4)
    lax.fori_loop(0, tm, drain, 0, unroll=4)
    wt = wt_ref[...]
    y = wt[:, 0:1] * buf_ref[0] + wt[:, 1:2] * buf_ref[1]
    xn = x_ref[...] + gt_ref[...] * y
    if final:
        ms = jnp.mean(xn * xn, axis=-1, keepdims=True)
        o_ref[...] = xn * lax.rsqrt(ms + _EPS) * g_ref[...]
    else:
        o_ref[...] = xn
        h_ref[...] = _norm_mod(xn, g_ref[...], sc_ref[...], sh_ref[...]).astype(h_ref.dtype)


def _moe_combine_call(pos, x, wt, modl, gate_idx, ys, seq, post_g, post_modl):
    t, d = x.shape
    tm = _pick(seq, 128)
    tpb = seq // tm
    final = post_modl is None
    if final:
        post_modl = modl
    row_spec = pl.BlockSpec((tm, d), lambda i, pos: (i, 0))
    out_specs = row_spec if final else [row_spec, row_spec]
    out_shape = (jax.ShapeDtypeStruct((t, d), F32) if final else
                 [jax.ShapeDtypeStruct((t, d), F32), jax.ShapeDtypeStruct((t, d), BF16)])
    return pl.pallas_call(
        functools.partial(_moe_combine_kernel, tm=tm, final=final),
        grid_spec=pltpu.PrefetchScalarGridSpec(
            num_scalar_prefetch=1,
            grid=(t // tm,),
            in_specs=[row_spec,
                      pl.BlockSpec((tm, 2), lambda i, pos: (i, 0)),
                      pl.BlockSpec((None, None, 1, d), lambda i, pos: (i // tpb, gate_idx, 0, 0)),
                      pl.BlockSpec((1, d), lambda i, pos: (0, 0)),
                      pl.BlockSpec((None, None, 1, d), lambda i, pos: (i // tpb, 1, 0, 0)),
                      pl.BlockSpec((None, None, 1, d), lambda i, pos: (i // tpb, 0, 0, 0)),
                      pl.BlockSpec(memory_space=pl.ANY)],
            out_specs=out_specs,
            scratch_shapes=[pltpu.VMEM((2, tm, d), F32), pltpu.SemaphoreType.DMA(())]),
        out_shape=out_shape,
        compiler_params=_params(("arbitrary",)),
        name="moe_combine",
    )(pos, x, wt, modl, post_g.astype(F32).reshape(1, d), post_modl, post_modl, ys)


def _route_tables(idx, n_exp, tm):
    n_sel = idx.shape[0] * idx.shape[1]
    e_flat = idx.reshape(-1)
    onehot = (e_flat[:, None] == jnp.arange(n_exp, dtype=jnp.int32)[None, :]).astype(jnp.int32)
    csum = jnp.cumsum(onehot, axis=0)
    counts = csum[-1]
    rank = jnp.sum(onehot * (csum - 1), axis=1)
    padded = ((counts + tm - 1) // tm) * tm
    ends = jnp.cumsum(padded)
    starts = ends - padded
    pos = jnp.sum(onehot * starts[None, :], axis=1) + rank
    n_rows = -(-n_sel // tm) * tm + n_exp * tm
    src = jnp.zeros((n_rows,), jnp.int32).at[pos].set(
        jnp.arange(n_sel, dtype=jnp.int32) // idx.shape[1])
    n_tiles = n_rows // tm
    n_used = (ends[-1] // tm).astype(jnp.int32)
    tile = jnp.minimum(jnp.arange(n_tiles, dtype=jnp.int32), n_used - 1)
    te = jnp.sum((tile[:, None] * tm >= ends[None, :]).astype(jnp.int32), axis=1)
    te = jnp.minimum(te, n_exp - 1).astype(jnp.int32)
    return pos.astype(jnp.int32), src, te, n_used.reshape(1), n_rows


def _moe_block(x, norm_g, modl, seq, router_w, router_bias, w_gate, w_up, w_down, layer,
               post_g, post_modl):
    n_exp = router_w.shape[1]
    h, idx_t, wt_t = _moe_norm_router_call(x, norm_g, modl, seq, router_w, router_bias)
    idx = idx_t[:2].T
    wt = wt_t[:2].T
    tm = 256
    pos, src, te, nu, n_rows = _route_tables(idx, n_exp, tm)
    xs = _gather_rows_call(src, nu, h, n_rows, tm)
    hmid = _moe_up_call(te, nu, xs, w_gate, w_up, layer, tm)
    ys = _moe_down_call(te, nu, hmid, w_down, layer, tm)
    return _moe_combine_call(pos, x, wt, modl, 5, ys, seq, post_g, post_modl)


def _gdn_mixer(x, h, modl, bsz, seq, w_in, conv_w, a_log, dt_bias, gnorm_g, w_out):
    d = x.shape[1]
    hv = a_log.shape[0]
    dh = gnorm_g.shape[0]
    conv_dim = conv_w.shape[1]
    value_dim = hv * dh
    key_dim = (conv_dim - value_dim) // 2
    hk = key_dim // dh
    assert w_in.shape[1] == conv_dim + value_dim + 2 * hv
    w_in_t = w_in.T
    proj = _mm_wt(h, w_in_t, conv_dim + value_dim, BF16)
    bg, bgt = _gdn_ba_call(h, w_in_t[conv_dim + value_dim:], a_log, dt_bias, hv)
    qkv = _dwconv_silu_call(proj, conv_dim, conv_w, seq)
    o = _delta_call(qkv, proj, bg, bgt.reshape(_LANES, 1, -1), gnorm_g, bsz, seq, hk, hv, dh)
    return _mm_resid(o, w_out, jnp.zeros((d,), F32), x, modl, 2, seq)


def _conformer_mixer(x, h, modl, seq, w_in, b_in, dw_w, dw_b, ln_g, ln_b, w_out, b_out):
    inner = w_in.shape[1] // 2
    u = _mm_glu(h, w_in, b_in, BF16)
    u = _dwconv_ln_call(u, dw_w, dw_b, ln_g, ln_b, seq)
    return _mm_resid(u, w_out, b_out, x, modl, 2, seq)


def kernel(x, c, ada_w, ada_b, norm_g, gdn_w_in, gdn_conv_w, gdn_a_log, gdn_dt_bias, gdn_norm_g, gdn_w_out, conf_w_in, conf_b_in, conf_dw_w, conf_dw_b, conf_ln_g, conf_ln_b, conf_w_out, conf_b_out, router_w, router_bias, moe_w_gate, moe_w_up, moe_w_down, final_norm_g):
    bsz, seq, d = x.shape
    depth = ada_w.shape[0]
    mod = _adaln(c, ada_w, ada_b)
    xf = x.reshape(bsz * seq, d)
    h = _norm_mod_call(xf, norm_g[0, 0], mod[0], 1, 0, seq, BF16)
    for i in range(depth):
        j = i // 2
        if i % 2 == 0:
            xf = _gdn_mixer(xf, h, mod[i], bsz, seq, gdn_w_in[j], gdn_conv_w[j],
                            gdn_a_log[j], gdn_dt_bias[j], gdn_norm_g[j], gdn_w_out[j])
        else:
            xf = _conformer_mixer(xf, h, mod[i], seq, conf_w_in[j], conf_b_in[j],
                                  conf_dw_w[j], conf_dw_b[j], conf_ln_g[j], conf_ln_b[j],
                                  conf_w_out[j], conf_b_out[j])
        last = i + 1 == depth
        out = _moe_block(xf, norm_g[i, 1], mod[i], seq, router_w, router_bias,
                         moe_w_gate, moe_w_up, moe_w_down, i,
                         final_norm_g if last else norm_g[i + 1, 0],
                         None if last else mod[i + 1])
        if last:
            return out.reshape(bsz, seq, d)
        xf, h = out
```

```python
import functools

import jax
import jax.numpy as jnp
from jax import lax
from jax.experimental import pallas as pl
from jax.experimental.pallas import tpu as pltpu

F32 = jnp.float32
BF16 = jnp.bfloat16

_EPS = 1e-6
_CHUNK = 64
_N_GROUPS = 4
_N_MOD = 6
_LANES = 128
_VMEM_LIMIT = 56 * 1024 * 1024


def _pick(dim, pref):
    t = min(pref, dim)
    while dim % t:
        t //= 2
    return max(t, 1)


def _params(sem, vmem=_VMEM_LIMIT):
    return pltpu.CompilerParams(dimension_semantics=sem, vmem_limit_bytes=vmem)


def _sigmoid(x):
    return 1.0 / (1.0 + jnp.exp(-x))


def _dot(a, b):
    return jnp.dot(a, b, preferred_element_type=F32)


def _dot_nt(a, b):
    return lax.dot_general(a, b, (((1,), (1,)), ((), ())), preferred_element_type=F32)


def _dot_tn(a, b):
    return lax.dot_general(a, b, (((0,), (0,)), ((), ())), preferred_element_type=F32)


def _adaln_kernel(c_ref, w_ref, b_ref, o_ref):
    c = c_ref[...]
    ca = (c * _sigmoid(c)).astype(BF16)
    o_ref[...] = _dot(ca, w_ref[...].astype(BF16)) + b_ref[...]


def _adaln(c, ada_w, ada_b):
    depth, d, n = ada_w.shape
    bsz = c.shape[0]
    bp = -(-bsz // 8) * 8
    c8 = jnp.pad(c, ((0, bp - bsz), (0, 0)))
    tn = _pick(n, 512)
    out = pl.pallas_call(
        _adaln_kernel,
        grid=(depth, n // tn),
        in_specs=[pl.BlockSpec((bp, d), lambda l, j: (0, 0)),
                  pl.BlockSpec((None, d, tn), lambda l, j: (l, 0, j)),
                  pl.BlockSpec((None, 1, tn), lambda l, j: (l, 0, j))],
        out_specs=pl.BlockSpec((None, bp, tn), lambda l, j: (l, 0, j)),
        out_shape=jax.ShapeDtypeStruct((depth, bp, n), F32),
        compiler_params=_params(("arbitrary", "arbitrary")),
        name="adaln",
    )(c8, ada_w, ada_b.reshape(depth, 1, n))
    return out[:, :bsz].reshape(depth, bsz, _N_MOD, 1, d)


def _norm_mod(x, g, sc, sh):
    ms = jnp.mean(x * x, axis=-1, keepdims=True)
    return x * lax.rsqrt(ms + _EPS) * g * (1.0 + sc) + sh


def _norm_mod_kernel(x_ref, g_ref, sc_ref, sh_ref, o_ref):
    o_ref[...] = _norm_mod(x_ref[...], g_ref[...], sc_ref[...], sh_ref[...]).astype(o_ref.dtype)


def _norm_mod_call(x, g, modl, sc_idx, sh_idx, seq, out_dtype):
    t, d = x.shape
    tm = _pick(seq, 256)
    tpb = seq // tm
    return pl.pallas_call(
        _norm_mod_kernel,
        grid=(t // tm,),
        in_specs=[pl.BlockSpec((tm, d), lambda i: (i, 0)),
                  pl.BlockSpec((1, d), lambda i: (0, 0)),
                  pl.BlockSpec((None, None, 1, d), lambda i: (i // tpb, sc_idx, 0, 0)),
                  pl.BlockSpec((None, None, 1, d), lambda i: (i // tpb, sh_idx, 0, 0))],
        out_specs=pl.BlockSpec((tm, d), lambda i: (i, 0)),
        out_shape=jax.ShapeDtypeStruct((t, d), out_dtype),
        compiler_params=_params(("arbitrary",)),
        name="norm_mod",
    )(x, g.reshape(1, d), modl, modl)


def _top2_sum(v):
    a = jnp.maximum(v[0], v[1]); b = jnp.minimum(v[0], v[1])
    c = jnp.maximum(v[2], v[3]); d = jnp.minimum(v[2], v[3])
    return jnp.maximum(a, c) + jnp.maximum(jnp.minimum(a, c), jnp.maximum(b, d))


def _first_argmax(vals):
    best = vals[0]
    idx = jnp.zeros(best.shape, jnp.int32)
    for e in range(1, len(vals)):
        take = vals[e] > best
        idx = jnp.where(take, e, idx)
        best = jnp.where(take, vals[e], best)
    return idx, best


def _moe_norm_router_kernel(x_ref, g_ref, sc_ref, sh_ref, rwh_ref, rwl_ref, rb_ref,
                            h_ref, idx_ref, wt_ref, *, n_exp):
    h = _norm_mod(x_ref[...], g_ref[...], sc_ref[...], sh_ref[...])
    h_ref[...] = h
    h_hi = h.astype(BF16)
    h_lo = (h - h_hi.astype(F32)).astype(BF16)
    logits = (_dot(h_hi, rwh_ref[...])
              + (_dot(h_lo, rwh_ref[...]) + _dot(h_hi, rwl_ref[...])))
    lt = logits.T
    per_group = n_exp // _N_GROUPS
    aff = [_sigmoid(lt[e:e + 1, :]) for e in range(n_exp)]
    biased = [aff[e] + rb_ref[e] for e in range(n_exp)]
    gscore = [_top2_sum(biased[gi * per_group:(gi + 1) * per_group]) for gi in range(_N_GROUPS)]
    best_group, _ = _first_argmax(gscore)
    neg = jnp.full(aff[0].shape, -jnp.inf, F32)
    masked = [jnp.where(best_group == (e // per_group), biased[e], neg) for e in range(n_exp)]
    i1, _ = _first_argmax(masked)
    masked2 = [jnp.where(i1 == e, neg, masked[e]) for e in range(n_exp)]
    i2, _ = _first_argmax(masked2)
    zero = jnp.zeros(aff[0].shape, F32)
    a1 = zero
    a2 = zero
    for e in range(n_exp):
        a1 = a1 + jnp.where(i1 == e, aff[e], zero)
        a2 = a2 + jnp.where(i2 == e, aff[e], zero)
    den = a1 + a2
    idx_ref[...] = jnp.zeros(idx_ref.shape, jnp.int32)
    wt_ref[...] = jnp.zeros(wt_ref.shape, F32)
    idx_ref[0:1, :] = i1
    idx_ref[1:2, :] = i2
    wt_ref[0:1, :] = a1 / den
    wt_ref[1:2, :] = a2 / den


def _moe_norm_router_call(x, g, modl, seq, router_w, router_bias):
    t, d = x.shape
    n_exp = router_w.shape[1]
    assert n_exp % _N_GROUPS == 0 and n_exp // _N_GROUPS == 4 and n_exp <= _LANES
    tm = _pick(seq, 256)
    tpb = seq // tm
    rw = jnp.pad(router_w.astype(F32), ((0, 0), (0, _LANES - n_exp)))
    rw_hi = rw.astype(BF16)
    rw_lo = (rw - rw_hi.astype(F32)).astype(BF16)
    return pl.pallas_call(
        functools.partial(_moe_norm_router_kernel, n_exp=n_exp),
        grid=(t // tm,),
        in_specs=[pl.BlockSpec((tm, d), lambda i: (i, 0)),
                  pl.BlockSpec((1, d), lambda i: (0, 0)),
                  pl.BlockSpec((None, None, 1, d), lambda i: (i // tpb, 4, 0, 0)),
                  pl.BlockSpec((None, None, 1, d), lambda i: (i // tpb, 3, 0, 0)),
                  pl.BlockSpec((d, _LANES), lambda i: (0, 0)),
                  pl.BlockSpec((d, _LANES), lambda i: (0, 0)),
                  pl.BlockSpec(memory_space=pltpu.SMEM)],
        out_specs=[pl.BlockSpec((tm, d), lambda i: (i, 0)),
                   pl.BlockSpec((8, tm), lambda i: (0, i)),
                   pl.BlockSpec((8, tm), lambda i: (0, i))],
        out_shape=[jax.ShapeDtypeStruct((t, d), F32),
                   jax.ShapeDtypeStruct((8, t), jnp.int32),
                   jax.ShapeDtypeStruct((8, t), F32)],
        compiler_params=_params(("arbitrary",)),
        name="moe_norm_router",
    )(x, g.reshape(1, d), modl, modl, rw_hi, rw_lo, router_bias.astype(F32))


def _mm_wt_kernel(x_ref, wt_ref, o_ref, wb_ref):
    @pl.when(pl.program_id(1) == 0)
    def _():
        wb_ref[...] = wt_ref[...].T.astype(BF16)
    o_ref[...] = _dot(x_ref[...], wb_ref[...]).astype(o_ref.dtype)


def _mm_wt(x, wt, n_out, out_dtype, tm_pref=1024, tn_pref=512):
    m, k = x.shape
    tm = _pick(m, tm_pref)
    tn = _pick(n_out, tn_pref)
    return pl.pallas_call(
        _mm_wt_kernel,
        grid=(n_out // tn, m // tm),
        in_specs=[pl.BlockSpec((tm, k), lambda j, i: (i, 0)),
                  pl.BlockSpec((tn, k), lambda j, i: (j, 0))],
        out_specs=pl.BlockSpec((tm, tn), lambda j, i: (i, j)),
        out_shape=jax.ShapeDtypeStruct((m, n_out), out_dtype),
        scratch_shapes=[pltpu.VMEM((k, tn), BF16)],
        compiler_params=_params(("arbitrary", "arbitrary")),
        name="mm_wt",
    )(x, wt)


def _mm_resid_kernel(x_ref, w_ref, b_ref, res_ref, gt_ref, o_ref, wb_ref):
    @pl.when(pl.program_id(1) == 0)
    def _():
        wb_ref[...] = w_ref[...].astype(BF16)
    y = _dot(x_ref[...], wb_ref[...]) + b_ref[...]
    o_ref[...] = res_ref[...] + gt_ref[...] * y


def _mm_resid(x, w, bias, res, modl, gate_idx, seq, tm_pref=1024, tn_pref=512):
    m, k = x.shape
    n = w.shape[1]
    tm = _pick(seq, tm_pref)
    tn = _pick(n, tn_pref)
    tpb = seq // tm
    return pl.pallas_call(
        _mm_resid_kernel,
        grid=(n // tn, m // tm),
        in_specs=[pl.BlockSpec((tm, k), lambda j, i: (i, 0)),
                  pl.BlockSpec((k, tn), lambda j, i: (0, j)),
                  pl.BlockSpec((1, tn), lambda j, i: (0, j)),
                  pl.BlockSpec((tm, tn), lambda j, i: (i, j)),
                  pl.BlockSpec((None, None, 1, tn), lambda j, i: (i // tpb, gate_idx, 0, j))],
        out_specs=pl.BlockSpec((tm, tn), lambda j, i: (i, j)),
        out_shape=jax.ShapeDtypeStruct((m, n), F32),
        scratch_shapes=[pltpu.VMEM((k, tn), BF16)],
        compiler_params=_params(("arbitrary", "arbitrary")),
        name="mm_resid",
    )(x, w, bias.reshape(1, n), res, modl)


def _mm_glu_kernel(x_ref, wv_ref, wg_ref, bv_ref, bg_ref, o_ref, wvb_ref, wgb_ref):
    @pl.when(pl.program_id(1) == 0)
    def _():
        wvb_ref[...] = wv_ref[...].astype(BF16)
        wgb_ref[...] = wg_ref[...].astype(BF16)
    x = x_ref[...]
    val = _dot(x, wvb_ref[...]) + bv_ref[...]
    gate = _dot(x, wgb_ref[...]) + bg_ref[...]
    o_ref[...] = (val * _sigmoid(gate)).astype(o_ref.dtype)


def _mm_glu(x, w, b, out_dtype, tm_pref=1024, tn_pref=256):
    m, k = x.shape
    n = w.shape[1] // 2
    tm = _pick(m, tm_pref)
    tn = _pick(n, tn_pref)
    nb = n // tn
    b2 = b.reshape(1, 2 * n)
    return pl.pallas_call(
        _mm_glu_kernel,
        grid=(nb, m // tm),
        in_specs=[pl.BlockSpec((tm, k), lambda j, i: (i, 0)),
                  pl.BlockSpec((k, tn), lambda j, i: (0, j)),
                  pl.BlockSpec((k, tn), lambda j, i: (0, j + nb)),
                  pl.BlockSpec((1, tn), lambda j, i: (0, j)),
                  pl.BlockSpec((1, tn), lambda j, i: (0, j + nb))],
        out_specs=pl.BlockSpec((tm, tn), lambda j, i: (i, j)),
        out_shape=jax.ShapeDtypeStruct((m, n), out_dtype),
        scratch_shapes=[pltpu.VMEM((k, tn), BF16), pltpu.VMEM((k, tn), BF16)],
        compiler_params=_params(("arbitrary", "arbitrary")),
        name="mm_glu",
    )(x, w, w, b2, b2)


def _gdn_ba_kernel(h_ref, w_ref, al_ref, dt_ref, o_ref, ot_ref, *, hv):
    p = _dot_nt(h_ref[...], w_ref[...].astype(BF16))
    tm = p.shape[0]
    lane = lax.broadcasted_iota(jnp.int32, p.shape, 1)
    beta = _sigmoid(p)
    a = p + dt_ref[...]
    softplus = jnp.maximum(a, 0.0) + jnp.log1p(jnp.exp(-jnp.abs(a)))
    g = -jnp.exp(al_ref[...]) * softplus
    ri = lax.broadcasted_iota(jnp.int32, (_CHUNK, _CHUNK), 0)
    ci = lax.broadcasted_iota(jnp.int32, (_CHUNK, _CHUNK), 1)
    tril = (ri >= ci).astype(F32)
    parts = [jnp.dot(tril, g[c * _CHUNK:(c + 1) * _CHUNK], preferred_element_type=F32,
                     precision=lax.Precision.HIGHEST) for c in range(tm // _CHUNK)]
    gcum = jnp.concatenate(parts, axis=0)
    out = jnp.where(lane < hv, beta, jnp.where(lane < 2 * hv, gcum, 0.0))
    o_ref[...] = out
    ot_ref[...] = out.T


def _gdn_ba_call(h, w_ba_t, a_log, dt_bias, hv):
    t, d = h.shape
    assert 2 * hv <= _LANES
    pad = _LANES - 2 * hv
    w128 = jnp.pad(w_ba_t, ((0, pad), (0, 0)))
    al = jnp.pad(a_log.astype(F32), (hv, pad)).reshape(1, _LANES)
    dt = jnp.pad(dt_bias.astype(F32), (hv, pad)).reshape(1, _LANES)
    tm = _pick(t, 512)
    assert tm % _CHUNK == 0
    return pl.pallas_call(
        functools.partial(_gdn_ba_kernel, hv=hv),
        grid=(t // tm,),
        in_specs=[pl.BlockSpec((tm, d), lambda i: (i, 0)),
                  pl.BlockSpec((_LANES, d), lambda i: (0, 0)),
                  pl.BlockSpec((1, _LANES), lambda i: (0, 0)),
                  pl.BlockSpec((1, _LANES), lambda i: (0, 0))],
        out_specs=[pl.BlockSpec((tm, _LANES), lambda i: (i, 0)),
                   pl.BlockSpec((_LANES, tm), lambda i: (0, i))],
        out_shape=[jax.ShapeDtypeStruct((t, _LANES), F32),
                   jax.ShapeDtypeStruct((_LANES, t), F32)],
        compiler_params=_params(("arbitrary",)),
        name="gdn_ba",
    )(h, w128, al, dt)


def _conv_halo_rows(width):
    return 16 * (-(-(width - 1) // 16))


def _fill_conv_buffer(buf_ref, cur_ref, halo_ref, halo, tpb):
    ts = cur_ref.shape[0]
    first = (pl.program_id(0) % tpb) == 0
    buf_ref[0:halo, :] = jnp.where(first, 0.0, halo_ref[...].astype(F32))
    buf_ref[halo:halo + ts, :] = cur_ref[...].astype(F32)


def _dwconv_silu_kernel(cur_ref, halo_ref, w_ref, o_ref, buf_ref, *, width, halo, tpb, rows):
    ts = cur_ref.shape[0]
    _fill_conv_buffer(buf_ref, cur_ref, halo_ref, halo, tpb)
    base = halo - (width - 1)
    for r0 in range(0, ts, rows):
        acc = None
        for kk in range(width):
            term = buf_ref[base + r0 + kk:base + r0 + kk + rows, :] * w_ref[kk:kk + 1, :]
            acc = term if acc is None else acc + term
        o_ref[r0:r0 + rows, :] = (acc * _sigmoid(acc)).astype(o_ref.dtype)


def _dwconv_silu_call(x, n_ch, w, seq, ts_pref=512, tc_pref=512, rows=64):
    t = x.shape[0]
    width = w.shape[0]
    halo = _conv_halo_rows(width)
    ts = _pick(seq, ts_pref)
    tc = _pick(n_ch, tc_pref)
    rows = min(rows, ts)
    assert ts % halo == 0 and ts % rows == 0
    tpb = seq // ts
    hb = ts // halo
    return pl.pallas_call(
        functools.partial(_dwconv_silu_kernel, width=width, halo=halo, tpb=tpb, rows=rows),
        grid=(t // ts, n_ch // tc),
        in_specs=[pl.BlockSpec((ts, tc), lambda i, j: (i, j)),
                  pl.BlockSpec((halo, tc), lambda i, j: (jnp.maximum(i * hb - 1, 0), j)),
                  pl.BlockSpec((width, tc), lambda i, j: (0, j))],
        out_specs=pl.BlockSpec((ts, tc), lambda i, j: (i, j)),
        out_shape=jax.ShapeDtypeStruct((t, n_ch), BF16),
        scratch_shapes=[pltpu.VMEM((halo + ts, tc), F32)],
        compiler_params=_params(("arbitrary", "arbitrary")),
        name="dwconv_silu",
    )(x, x, w.astype(F32))


def _dwconv_ln_kernel(cur_ref, halo_ref, w_ref, b_ref, lg_ref, lb_ref, o_ref, buf_ref, sh_ref,
                      acc_ref, *, width, halo, tpb, rows, cw):
    ts, nch = cur_ref.shape
    _fill_conv_buffer(buf_ref, cur_ref, halo_ref, halo, tpb)
    base = halo - (width - 1)
    n_cc = nch // cw
    n_sh = halo + ts - 8

    def shift_block(cc, carry):
        c0 = pl.multiple_of(cc * cw, cw)
        blk = buf_ref[:, pl.ds(c0, cw)]
        for s in range(1, 8):
            sh_ref[s - 1, :, pl.ds(c0, cw)] = blk[s:s + n_sh]
        return carry

    lax.fori_loop(0, n_cc, shift_block, 0)

    def conv_block(it, carry):
        r0 = pl.multiple_of((it // n_cc) * rows, rows)
        c0 = pl.multiple_of((it % n_cc) * cw, cw)
        acc = None
        for kk in range(width):
            a, s = divmod(base + kk, 8)
            if s == 0:
                win = buf_ref[pl.ds(r0 + 8 * a, rows), pl.ds(c0, cw)]
            else:
                win = sh_ref[s - 1, pl.ds(r0 + 8 * a, rows), pl.ds(c0, cw)]
            term = win * w_ref[kk:kk + 1, pl.ds(c0, cw)]
            acc = term if acc is None else acc + term
        acc_ref[pl.ds(r0, rows), pl.ds(c0, cw)] = acc + b_ref[:, pl.ds(c0, cw)]
        return carry

    lax.fori_loop(0, (ts // rows) * n_cc, conv_block, 0)

    ln_rows = 16

    def norm_block(it, carry):
        r0 = pl.multiple_of(it * ln_rows, ln_rows)
        rows_ref = acc_ref.at[pl.ds(r0, ln_rows), :]
        mu = jnp.mean(rows_ref[...], axis=-1, keepdims=True)
        var = jnp.mean(jnp.square(rows_ref[...] - mu), axis=-1, keepdims=True)
        scale = lax.rsqrt(var + _EPS)
        for c0 in range(0, nch, cw):
            y = ((rows_ref[:, c0:c0 + cw] - mu) * scale * lg_ref[:, c0:c0 + cw]
                 + lb_ref[:, c0:c0 + cw])
            o_ref[pl.ds(r0, ln_rows), c0:c0 + cw] = (y * _sigmoid(y)).astype(o_ref.dtype)
        return carry

    lax.fori_loop(0, ts // ln_rows, norm_block, 0, unroll=4)


def _dwconv_ln_call(x, w, bias, ln_g, ln_b, seq, ts_pref=128, rows=128, cw=256):
    t, n_ch = x.shape
    width = w.shape[0]
    halo = _conv_halo_rows(width)
    ts = _pick(seq, ts_pref)
    cw = _pick(n_ch, cw)
    rows = min(rows, ts)
    assert ts % halo == 0 and ts % rows == 0 and ts % 16 == 0
    tpb = seq // ts
    hb = ts // halo

    def row(v):
        return v.astype(F32).reshape(1, n_ch)

    return pl.pallas_call(
        functools.partial(_dwconv_ln_kernel, width=width, halo=halo, tpb=tpb, rows=rows, cw=cw),
        grid=(t // ts,),
        in_specs=[pl.BlockSpec((ts, n_ch), lambda i: (i, 0)),
                  pl.BlockSpec((halo, n_ch), lambda i: (jnp.maximum(i * hb - 1, 0), 0)),
                  pl.BlockSpec((width, n_ch), lambda i: (0, 0)),
                  pl.BlockSpec((1, n_ch), lambda i: (0, 0)),
                  pl.BlockSpec((1, n_ch), lambda i: (0, 0)),
                  pl.BlockSpec((1, n_ch), lambda i: (0, 0))],
        out_specs=pl.BlockSpec((ts, n_ch), lambda i: (i, 0)),
        out_shape=jax.ShapeDtypeStruct((t, n_ch), BF16),
        scratch_shapes=[pltpu.VMEM((halo + ts, n_ch), F32),
                        pltpu.VMEM((7, halo + ts - 8, n_ch), F32),
                        pltpu.VMEM((ts, n_ch), F32)],
        compiler_params=_params(("arbitrary",)),
        name="dwconv_ln",
    )(x, x, w.astype(F32), row(bias), row(ln_g), row(ln_b))


def _delta_kernel(q_ref, k_ref, v_ref, z_ref, bg_ref, grow_ref, ng_ref, o_ref, s_ref,
                  *, hb, rep, dh, hv_total, scale):
    @pl.when(pl.program_id(2) == 0)
    def _():
        s_ref[...] = jnp.zeros(s_ref.shape, F32)

    ts = q_ref.shape[0]
    nc = ts // _CHUNK
    bg = bg_ref[...]
    lane = lax.broadcasted_iota(jnp.int32, bg.shape, 1)
    ri = lax.broadcasted_iota(jnp.int32, (ts, ts), 0)
    ci = lax.broadcasted_iota(jnp.int32, (ts, ts), 1)
    same = (ri // _CHUNK) == (ci // _CHUNK)
    causal = jnp.logical_and(same, ri >= ci)
    strict = jnp.logical_and(same, ri > ci)
    ng = ng_ref[...]
    n_fac = _CHUNK.bit_length() - 1
    heads = [(hl, r) for hl in range(hb) for r in range(rep)]
    nh = len(heads)

    qn_l, kn_l, qk_l, kk_l = [], [], [], []
    for hl in range(hb):
        q = q_ref[:, hl * dh:(hl + 1) * dh].astype(F32)
        k = k_ref[:, hl * dh:(hl + 1) * dh].astype(F32)
        qn = q * lax.rsqrt(jnp.sum(q * q, axis=-1, keepdims=True) + _EPS) * scale
        kn = k * lax.rsqrt(jnp.sum(k * k, axis=-1, keepdims=True) + _EPS)
        kb = kn.astype(BF16)
        qkk = _dot_nt(jnp.concatenate([qn.astype(BF16), kb], axis=0), kb)
        qn_l.append(qn)
        kn_l.append(kn)
        qk_l.append(qkk[:ts])
        kk_l.append(qkk[ts:])

    gcol_l, decay_l, eg_l, q_pow, r_acc = [], [], [], [], []
    for hi, (hl, r) in enumerate(heads):
        hv = (pl.program_id(1) * hb + hl) * rep + r
        bcol = jnp.sum(jnp.where(lane == hv, bg, 0.0), axis=-1, keepdims=True)
        gcol = jnp.sum(jnp.where(lane == hv_total + hv, bg, 0.0), axis=-1, keepdims=True)
        grow = grow_ref[hi]
        v = v_ref[:, hi * dh:(hi + 1) * dh].astype(F32)
        decay = jnp.where(causal, jnp.exp(jnp.where(causal, gcol - grow, 0.0)), 0.0)
        eg = jnp.exp(gcol)
        gcol_l.append(gcol)
        decay_l.append(decay)
        eg_l.append(eg)
        q_pow.append(-jnp.where(strict, kk_l[hl] * bcol * decay, 0.0))
        r_acc.append(jnp.concatenate([v * bcol, kn_l[hl] * (bcol * eg)], axis=1))

    eye = (ri == ci).astype(F32)
    t_inv = [eye + q_pow[hi] for hi in range(nh)]
    for hi in range(nh):
        qb = q_pow[hi].astype(BF16)
        q_pow[hi] = _dot(qb, qb)
    for i in range(1, n_fac):
        for hi in range(nh):
            qb = q_pow[hi].astype(BF16)
            pb = t_inv[hi].astype(BF16)
            if i + 1 < n_fac:
                y = _dot(jnp.concatenate([pb, qb], axis=0), qb)
                t_inv[hi] = t_inv[hi] + y[:ts]
                q_pow[hi] = y[ts:]
            else:
                t_inv[hi] = t_inv[hi] + _dot(pb, qb)
    for hi in range(nh):
        r_acc[hi] = _dot(t_inv[hi].astype(BF16), r_acc[hi].astype(BF16))

    sol_l, qeff_l, oloc_l, kdec_l = [], [], [], []
    for hi, (hl, r) in enumerate(heads):
        sol = r_acc[hi].astype(BF16)
        qx = _dot((qk_l[hl] * decay_l[hi]).astype(BF16), sol)
        gcol = gcol_l[hi]
        glast = jnp.concatenate(
            [jnp.broadcast_to(gcol[(c + 1) * _CHUNK - 1:(c + 1) * _CHUNK, :], (_CHUNK, 1))
             for c in range(nc)], axis=0)
        sol_l.append(sol)
        qeff_l.append((qn_l[hl] * eg_l[hi] - qx[:, dh:]).astype(BF16))
        oloc_l.append(qx[:, :dh])
        kdec_l.append((kn_l[hl] * jnp.exp(glast - gcol)).astype(BF16))

    states = [s_ref[hi] for hi in range(nh)]
    for c in range(nc):
        sl = slice(c * _CHUNK, (c + 1) * _CHUNK)
        for hi in range(nh):
            kx = _dot_tn(kdec_l[hi][sl], sol_l[hi][sl])
            sb = states[hi].astype(BF16)
            y = _dot(jnp.concatenate([qeff_l[hi][sl], kx[:, dh:].astype(BF16)], axis=0), sb)
            o = y[:_CHUNK] + oloc_l[hi][sl]
            gl = jnp.exp(gcol_l[hi][(c + 1) * _CHUNK - 1:(c + 1) * _CHUNK, :])
            states[hi] = gl * states[hi] - y[_CHUNK:] + kx[:, :dh]
            on = o * lax.rsqrt(jnp.mean(o * o, axis=-1, keepdims=True) + _EPS) * ng
            zc = z_ref[sl, hi * dh:(hi + 1) * dh].astype(F32)
            o_ref[sl, hi * dh:(hi + 1) * dh] = (on * (zc * _sigmoid(zc))).astype(o_ref.dtype)
    for hi in range(nh):
        s_ref[hi] = states[hi]


def _delta_call(qkv, proj, bg, bgt3, norm_g, bsz, seq, hk, hv, dh, hb=8, ts_pref=128):
    t = qkv.shape[0]
    rep = hv // hk
    hb = min(hb, hk)
    assert hv % hk == 0 and hk % hb == 0 and (2 * hk) % (hb * rep) == 0 and dh % _LANES == 0
    conv_dim = qkv.shape[1]
    ts = _pick(seq, ts_pref)
    assert ts % _CHUNK == 0
    ns = seq // ts
    nh = hb * rep
    v_blk0 = (2 * hk) // nh
    z_blk0 = conv_dim // (nh * dh)
    return pl.pallas_call(
        functools.partial(_delta_kernel, hb=hb, rep=rep, dh=dh, hv_total=hv,
                          scale=float(dh) ** -0.5),
        grid=(bsz, hk // hb, ns),
        in_specs=[pl.BlockSpec((ts, hb * dh), lambda b, h, s: (b * ns + s, h)),
                  pl.BlockSpec((ts, hb * dh), lambda b, h, s: (b * ns + s, hk // hb + h)),
                  pl.BlockSpec((ts, nh * dh), lambda b, h, s: (b * ns + s, v_blk0 + h)),
                  pl.BlockSpec((ts, nh * dh), lambda b, h, s: (b * ns + s, z_blk0 + h)),
                  pl.BlockSpec((ts, _LANES), lambda b, h, s: (b * ns + s, 0)),
                  pl.BlockSpec((nh, 1, ts), lambda b, h, s: (hv // nh + h, 0, b * ns + s)),
                  pl.BlockSpec((1, dh), lambda b, h, s: (0, 0))],
        out_specs=pl.BlockSpec((ts, nh * dh), lambda b, h, s: (b * ns + s, h)),
        out_shape=jax.ShapeDtypeStruct((t, hv * dh), BF16),
        scratch_shapes=[pltpu.VMEM((nh, dh, dh), F32)],
        compiler_params=_params(("arbitrary", "arbitrary", "arbitrary")),
        name="gdn_delta",
    )(qkv, qkv, qkv, proj, bg, bgt3, norm_g.astype(F32).reshape(1, dh))


def _gather_rows_kernel(src_ref, nu_ref, x_hbm, o_ref, buf_ref, sem_ref, *, tm):
    t = pl.program_id(0)
    n_used = nu_ref[0]
    slot = t % 2

    def row_copy(tile, slot_, r):
        return pltpu.make_async_copy(x_hbm.at[pl.ds(src_ref[tile * tm + r], 1), :],
                                     buf_ref.at[slot_, pl.ds(r, 1), :], sem_ref.at[slot_])

    def issue(tile, slot_):
        def body(r, carry):
            row_copy(tile, slot_, r).start()
            return carry
        lax.fori_loop(0, tm, body, 0, unroll=8)

    @pl.when(jnp.logical_and(t == 0, n_used > 0))
    def _():
        issue(0, 0)

    @pl.when(t + 1 < n_used)
    def _():
        issue(t + 1, 1 - slot)

    @pl.when(t < n_used)
    def _():
        def body(r, carry):
            row_copy(t, slot, r).wait()
            return carry
        lax.fori_loop(0, tm, body, 0, unroll=8)
        o_ref[...] = buf_ref[slot].astype(o_ref.dtype)

    @pl.when(t >= n_used)
    def _():
        o_ref[...] = jnp.zeros(o_ref.shape, o_ref.dtype)


def _gather_rows_call(src, nu, x, n_rows, tm):
    d = x.shape[1]
    return pl.pallas_call(
        functools.partial(_gather_rows_kernel, tm=tm),
        grid_spec=pltpu.PrefetchScalarGridSpec(
            num_scalar_prefetch=2,
            grid=(n_rows // tm,),
            in_specs=[pl.BlockSpec(memory_space=pl.ANY)],
            out_specs=pl.BlockSpec((tm, d), lambda t, src, nu: (t, 0)),
            scratch_shapes=[pltpu.VMEM((2, tm, d), x.dtype), pltpu.SemaphoreType.DMA((2,))]),
        out_shape=jax.ShapeDtypeStruct((n_rows, d), BF16),
        compiler_params=_params(("arbitrary",)),
        name="moe_gather",
    )(src, nu, x)


def _expert_changed(te_ref, t):
    return jnp.logical_or(t == 0, te_ref[t] != te_ref[jnp.maximum(t - 1, 0)])


def _moe_up_kernel(te_ref, nu_ref, x_ref, wg_ref, wu_ref, o_ref, wgb_ref, wub_ref):
    t = pl.program_id(1)

    @pl.when(_expert_changed(te_ref, t))
    def _():
        wgb_ref[...] = wg_ref[...].astype(BF16)
        wub_ref[...] = wu_ref[...].astype(BF16)

    @pl.when(t < nu_ref[0])
    def _():
        x = x_ref[...]
        g = _dot(x, wgb_ref[...])
        u = _dot(x, wub_ref[...])
        o_ref[...] = (g * _sigmoid(g) * u).astype(o_ref.dtype)

    @pl.when(t >= nu_ref[0])
    def _():
        o_ref[...] = jnp.zeros(o_ref.shape, o_ref.dtype)


def _moe_up_call(te, nu, xs, w_gate, w_up, layer, tm, tf_pref=512):
    n_rows, d = xs.shape
    f = w_gate.shape[3]
    tf = _pick(f, tf_pref)
    return pl.pallas_call(
        _moe_up_kernel,
        grid_spec=pltpu.PrefetchScalarGridSpec(
            num_scalar_prefetch=2,
            grid=(f // tf, n_rows // tm),
            in_specs=[pl.BlockSpec((tm, d), lambda j, t, te, nu: (t, 0)),
                      pl.BlockSpec((None, None, d, tf), lambda j, t, te, nu: (layer, te[t], 0, j)),
                      pl.BlockSpec((None, None, d, tf), lambda j, t, te, nu: (layer, te[t], 0, j))],
            out_specs=pl.BlockSpec((tm, tf), lambda j, t, te, nu: (t, j)),
            scratch_shapes=[pltpu.VMEM((d, tf), BF16), pltpu.VMEM((d, tf), BF16)]),
        out_shape=jax.ShapeDtypeStruct((n_rows, f), BF16),
        compiler_params=_params(("arbitrary", "arbitrary")),
        name="moe_up",
    )(te, nu, xs, w_gate, w_up)


def _moe_down_kernel(te_ref, nu_ref, h_ref, wd_ref, o_ref, wdb_ref):
    t = pl.program_id(1)

    @pl.when(_expert_changed(te_ref, t))
    def _():
        wdb_ref[...] = wd_ref[...].astype(BF16)

    @pl.when(t < nu_ref[0])
    def _():
        o_ref[...] = _dot(h_ref[...], wdb_ref[...])

    @pl.when(t >= nu_ref[0])
    def _():
        o_ref[...] = jnp.zeros(o_ref.shape, o_ref.dtype)


def _moe_down_call(te, nu, hmid, w_down, layer, tm, tn_pref=4096):
    n_rows, f = hmid.shape
    d = w_down.shape[3]
    tn = _pick(d, tn_pref)
    return pl.pallas_call(
        _moe_down_kernel,
        grid_spec=pltpu.PrefetchScalarGridSpec(
            num_scalar_prefetch=2,
            grid=(d // tn, n_rows // tm),
            in_specs=[pl.BlockSpec((tm, f), lambda j, t, te, nu: (t, 0)),
                      pl.BlockSpec((None, None, f, tn), lambda j, t, te, nu: (layer, te[t], 0, j))],
            out_specs=pl.BlockSpec((tm, tn), lambda j, t, te, nu: (t, j)),
            scratch_shapes=[pltpu.VMEM((f, tn), BF16)]),
        out_shape=jax.ShapeDtypeStruct((n_rows, d), F32),
        compiler_params=_params(("arbitrary", "arbitrary")),
        name="moe_down",
    )(te, nu, hmid, w_down)


def _moe_combine_kernel(pos_ref, x_ref, wt_ref, gt_ref, g_ref, sc_ref, sh_ref, ys_hbm, *rest,
                        tm, final):
    if final:
        o_ref, buf_ref, sem = rest
    else:
        o_ref, h_ref, buf_ref, sem = rest
    base = pl.program_id(0) * tm

    def row_copy(r, kk):
        return pltpu.make_async_copy(ys_hbm.at[pl.ds(pos_ref[(base + r) * 2 + kk], 1), :],
                                     buf_ref.at[kk, pl.ds(r, 1), :], sem)

    def issue(r, carry):
        row_copy(r, 0).start()
        row_copy(r, 1).start()
        return carry

    def drain(r, carry):
        row_copy(r, 0).wait()
        row_copy(r, 1).wait()
        return carry

    lax.fori_loop(0, tm, issue, 0, unroll=4)
    lax.fori_loop(0, tm, drain, 0, unroll=4)
    wt = wt_ref[...]
    y = wt[:, 0:1] * buf_ref[0] + wt[:, 1:2] * buf_ref[1]
    xn = x_ref[...] + gt_ref[...] * y
    if final:
        ms = jnp.mean(xn * xn, axis=-1, keepdims=True)
        o_ref[...] = xn * lax.rsqrt(ms + _EPS) * g_ref[...]
    else:
        o_ref[...] = xn
        h_ref[...] = _norm_mod(xn, g_ref[...], sc_ref[...], sh_ref[...]).astype(h_ref.dtype)


def _moe_combine_call(pos, x, wt, modl, gate_idx, ys, seq, post_g, post_modl):
    t, d = x.shape
    tm = _pick(seq, 128)
    tpb = seq // tm
    final = post_modl is None
    if final:
        post_modl = modl
    row_spec = pl.BlockSpec((tm, d), lambda i, pos: (i, 0))
    out_specs = row_spec if final else [row_spec, row_spec]
    out_shape = (jax.ShapeDtypeStruct((t, d), F32) if final else
                 [jax.ShapeDtypeStruct((t, d), F32), jax.ShapeDtypeStruct((t, d), BF16)])
    return pl.pallas_call(
        functools.partial(_moe_combine_kernel, tm=tm, final=final),
        grid_spec=pltpu.PrefetchScalarGridSpec(
            num_scalar_prefetch=1,
            grid=(t // tm,),
            in_specs=[row_spec,
                      pl.BlockSpec((tm, 2), lambda i, pos: (i, 0)),
                      pl.BlockSpec((None, None, 1, d), lambda i, pos: (i // tpb, gate_idx, 0, 0)),
                      pl.BlockSpec((1, d), lambda i, pos: (0, 0)),
                      pl.BlockSpec((None, None, 1, d), lambda i, pos: (i // tpb, 1, 0, 0)),
                      pl.BlockSpec((None, None, 1, d), lambda i, pos: (i // tpb, 0, 0, 0)),
                      pl.BlockSpec(memory_space=pl.ANY)],
            out_specs=out_specs,
            scratch_shapes=[pltpu.VMEM((2, tm, d), F32), pltpu.SemaphoreType.DMA(())]),
        out_shape=out_shape,
        compiler_params=_params(("arbitrary",)),
        name="moe_combine",
    )(pos, x, wt, modl, post_g.astype(F32).reshape(1, d), post_modl, post_modl, ys)


def _route_tables(idx, n_exp, tm):
    n_sel = idx.shape[0] * idx.shape[1]
    e_flat = idx.reshape(-1)
    onehot = (e_flat[:, None] == jnp.arange(n_exp, dtype=jnp.int32)[None, :]).astype(jnp.int32)
    csum = jnp.cumsum(onehot, axis=0)
    counts = csum[-1]
    rank = jnp.sum(onehot * (csum - 1), axis=1)
    padded = ((counts + tm - 1) // tm) * tm
    ends = jnp.cumsum(padded)
    starts = ends - padded
    pos = jnp.sum(onehot * starts[None, :], axis=1) + rank
    n_rows = -(-n_sel // tm) * tm + n_exp * tm
    src = jnp.zeros((n_rows,), jnp.int32).at[pos].set(
        jnp.arange(n_sel, dtype=jnp.int32) // idx.shape[1])
    n_tiles = n_rows // tm
    n_used = (ends[-1] // tm).astype(jnp.int32)
    tile = jnp.minimum(jnp.arange(n_tiles, dtype=jnp.int32), n_used - 1)
    te = jnp.sum((tile[:, None] * tm >= ends[None, :]).astype(jnp.int32), axis=1)
    te = jnp.minimum(te, n_exp - 1).astype(jnp.int32)
    return pos.astype(jnp.int32), src, te, n_used.reshape(1), n_rows


def _moe_block(x, norm_g, modl, seq, router_w, router_bias, w_gate, w_up, w_down, layer,
               post_g, post_modl):
    n_exp = router_w.shape[1]
    h, idx_t, wt_t = _moe_norm_router_call(x, norm_g, modl, seq, router_w, router_bias)
    idx = idx_t[:2].T
    wt = wt_t[:2].T
    tm = 256
    pos, src, te, nu, n_rows = _route_tables(idx, n_exp, tm)
    xs = _gather_rows_call(src, nu, h, n_rows, tm)
    hmid = _moe_up_call(te, nu, xs, w_gate, w_up, layer, tm)
    ys = _moe_down_call(te, nu, hmid, w_down, layer, tm)
    return _moe_combine_call(pos, x, wt, modl, 5, ys, seq, post_g, post_modl)


def _gdn_mixer(x, h, modl, bsz, seq, w_in, conv_w, a_log, dt_bias, gnorm_g, w_out):
    d = x.shape[1]
    hv = a_log.shape[0]
    dh = gnorm_g.shape[0]
    conv_dim = conv_w.shape[1]
    value_dim = hv * dh
    key_dim = (conv_dim - value_dim) // 2
    hk = key_dim // dh
    assert w_in.shape[1] == conv_dim + value_dim + 2 * hv
    w_in_t = w_in.T
    proj = _mm_wt(h, w_in_t, conv_dim + value_dim, BF16)
    bg, bgt = _gdn_ba_call(h, w_in_t[conv_dim + value_dim:], a_log, dt_bias, hv)
    qkv = _dwconv_silu_call(proj, conv_dim, conv_w, seq)
    o = _delta_call(qkv, proj, bg, bgt.reshape(_LANES, 1, -1), gnorm_g, bsz, seq, hk, hv, dh)
    return _mm_resid(o, w_out, jnp.zeros((d,), F32), x, modl, 2, seq)


def _conformer_mixer(x, h, modl, seq, w_in, b_in, dw_w, dw_b, ln_g, ln_b, w_out, b_out):
    inner = w_in.shape[1] // 2
    u = _mm_glu(h, w_in, b_in, BF16)
    u = _dwconv_ln_call(u, dw_w, dw_b, ln_g, ln_b, seq)
    return _mm_resid(u, w_out, b_out, x, modl, 2, seq)


def kernel(x, c, ada_w, ada_b, norm_g, gdn_w_in, gdn_conv_w, gdn_a_log, gdn_dt_bias, gdn_norm_g, gdn_w_out, conf_w_in, conf_b_in, conf_dw_w, conf_dw_b, conf_ln_g, conf_ln_b, conf_w_out, conf_b_out, router_w, router_bias, moe_w_gate, moe_w_up, moe_w_down, final_norm_g):
    bsz, seq, d = x.shape
    depth = ada_w.shape[0]
    mod = _adaln(c, ada_w, ada_b)
    xf = x.reshape(bsz * seq, d)
    h = _norm_mod_call(xf, norm_g[0, 0], mod[0], 1, 0, seq, BF16)
    for i in range(depth):
        j = i // 2
        if i % 2 == 0:
            xf = _gdn_mixer(xf, h, mod[i], bsz, seq, gdn_w_in[j], gdn_conv_w[j],
                            gdn_a_log[j], gdn_dt_bias[j], gdn_norm_g[j], gdn_w_out[j])
        else:
            xf = _conformer_mixer(xf, h, mod[i], seq, conf_w_in[j], conf_b_in[j],
                                  conf_dw_w[j], conf_dw_b[j], conf_ln_g[j], conf_ln_b[j],
                                  conf_w_out[j], conf_b_out[j])
        last = i + 1 == depth
        out = _moe_block(xf, norm_g[i, 1], mod[i], seq, router_w, router_bias,
                         moe_w_gate, moe_w_up, moe_w_down, i,
                         final_norm_g if last else norm_g[i + 1, 0],
                         None if last else mod[i + 1])
        if last:
            return out.reshape(bsz, seq, d)
        xf, h = out
```
